```python
import math
import jax, jax.numpy as jnp
from jax import lax
import numpy as np

D_MODEL = 1024
BATCH = 8
SEQ = 8192
DEPTH = 4

D_MIX = D_MODEL
POOL_W = D_MIX // 4
SCONV_W = D_MIX // 4
ATTN_W = D_MIX // 4
CONF_W = D_MIX // 4
POOL_WINDOWS = (2, 4, 8, 16)
N_POOL_GROUPS = len(POOL_WINDOWS)
POOL_GROUP = POOL_W // N_POOL_GROUPS
SCONV_K = 3
N_DIFF_HEADS = 4
DIFF_V_DIM = ATTN_W // N_DIFF_HEADS
DIFF_QK_DIM = DIFF_V_DIM // 2
ROPE_DIM = DIFF_QK_DIM // 4
ROPE_THETA = 500000.0
Q_BLOCK = 128
CONF_K = 31
D_FF = 2816
N_EXPERTS = 8
TOP_K = 2
D_FF_EXPERT = 3584
N_DENSE = (DEPTH + 1) // 2
N_MOE = DEPTH // 2
DN_ALPHA = (2 * DEPTH) ** 0.25
DN_BETA = (8 * DEPTH) ** -0.25
LN_EPS = 1e-5
IN_SIZES = (POOL_W,
            SCONV_W, SCONV_W, SCONV_W,
            ATTN_W, ATTN_W, ATTN_W,
            CONF_W, CONF_W)
IN_COLS = sum(IN_SIZES)

kernel_name = "hybrid_pool_sconv_diffattn_conformer_moe"


def layer_norm(x, g, b):
    xf = x.astype(jnp.float32)
    mu = jnp.mean(xf, axis=-1, keepdims=True)
    xc = xf - mu
    var = jnp.mean(xc * xc, axis=-1, keepdims=True)
    return (xc * lax.rsqrt(var + LN_EPS) * g.astype(jnp.float32) + b.astype(jnp.float32)).astype(x.dtype)


def causal_depthwise_conv(x, w):
    k, c = w.shape
    return lax.conv_general_dilated(
        x, w[:, None, :].astype(x.dtype), window_strides=(1,), padding=[(k - 1, 0)],
        dimension_numbers=('NWC', 'WIO', 'NWC'), feature_group_count=c)


def pool_mixer(h, w_pool, pool_scale):
    b, s, c = h.shape
    hf = h.astype(jnp.float32)
    cs = jnp.cumsum(hf, axis=1)
    tpos = jnp.arange(1, s + 1)
    outs = []
    for gi, w in enumerate(POOL_WINDOWS):
        sl = slice(gi * POOL_GROUP, (gi + 1) * POOL_GROUP)
        csg = cs[..., sl]
        lag = jnp.pad(csg, ((0, 0), (w, 0), (0, 0)))[:, :s]
        cnt = jnp.minimum(tpos, w).astype(jnp.float32)[None, :, None]
        outs.append((csg - lag) / cnt - hf[..., sl])
    d = jnp.stack(outs, axis=2).astype(h.dtype)
    y = jnp.einsum('bsgc,gce->bsge', d, w_pool).reshape(b, s, c)
    return y * pool_scale


def short_conv_mixer(gate_b, gate_c, h, w_conv):
    return gate_b * causal_depthwise_conv(gate_c * h, w_conv)


def rope_tables(positions):
    inv = ROPE_THETA ** (-jnp.arange(0, ROPE_DIM, 2, dtype=jnp.float32) / ROPE_DIM)
    ang = positions.astype(jnp.float32)[..., None] * inv
    return jnp.cos(ang), jnp.sin(ang)


def partial_rope(x, cos, sin):
    half = ROPE_DIM // 2
    xr = x[..., :ROPE_DIM].astype(jnp.float32)
    x1, x2 = xr[..., :half], xr[..., half:]
    c = cos[:, :, None, None, :]
    sn = sin[:, :, None, None, :]
    rot = jnp.concatenate([x1 * c - x2 * sn, x2 * c + x1 * sn], axis=-1)
    return jnp.concatenate([rot.astype(x.dtype), x[..., ROPE_DIM:]], axis=-1)


def diff_attention(q, k, v, lam, lam_init, subln_g):
    b, s = q.shape[:2]
    nb = s // Q_BLOCK
    qt = (q * (DIFF_QK_DIM ** -0.5)).transpose(0, 2, 3, 1, 4)
    kt = k.transpose(0, 2, 3, 1, 4)
    vt = v.transpose(0, 2, 1, 3)
    kpos = jnp.arange(s)

    def block(i):
        q0 = i * Q_BLOCK
        qb = lax.dynamic_slice_in_dim(qt, q0, Q_BLOCK, axis=3)
        sc = jnp.einsum('bhmqd,bhmkd->bhmqk', qb, kt).astype(jnp.float32)
        qpos = q0 + jnp.arange(Q_BLOCK)
        sc = jnp.where(kpos[None, :] <= qpos[:, None], sc, -jnp.inf)
        p = jax.nn.softmax(sc, axis=-1)
        a = p[:, :, 0] - lam * p[:, :, 1]
        return jnp.einsum('bhqk,bhke->bhqe', a.astype(v.dtype), vt)

    o = lax.map(block, jnp.arange(nb))
    o = o.transpose(1, 0, 3, 2, 4).reshape(b, s, N_DIFF_HEADS, DIFF_V_DIM)
    of = o.astype(jnp.float32)
    of = of * lax.rsqrt(jnp.mean(of * of, axis=-1, keepdims=True) + LN_EPS) * subln_g.astype(jnp.float32)
    return (of * (1.0 - lam_init)).reshape(b, s, ATTN_W).astype(v.dtype)


def conformer_conv(a, g, w_dw, ln_g, ln_b):
    u = a * jax.nn.sigmoid(g)
    u = causal_depthwise_conv(u, w_dw)
    u = layer_norm(u, ln_g, ln_b)
    return jax.nn.silu(u)


def hybrid_mixer(x, w_in, w_o, pool_w, pool_scale, sconv_w, lam_params, subln_g,
                 conf_dw, conf_ln_g, conf_ln_b, cos, sin, lam_init):
    b, s, _ = x.shape
    z = x @ w_in
    (p_in, sb, scg, sh, qf, kf, vf, ca, cg) = jnp.split(
        z, np.cumsum(IN_SIZES)[:-1].tolist(), axis=-1)
    y_pool = pool_mixer(p_in, pool_w, pool_scale)
    y_sconv = short_conv_mixer(sb, scg, sh, sconv_w)
    q = partial_rope(qf.reshape(b, s, N_DIFF_HEADS, 2, DIFF_QK_DIM), cos, sin)
    k = partial_rope(kf.reshape(b, s, N_DIFF_HEADS, 2, DIFF_QK_DIM), cos, sin)
    v = vf.reshape(b, s, N_DIFF_HEADS, DIFF_V_DIM)
    lp = lam_params.astype(jnp.float32)
    lam = jnp.exp(jnp.sum(lp[0] * lp[1])) - jnp.exp(jnp.sum(lp[2] * lp[3])) + lam_init
    y_attn = diff_attention(q, k, v, lam, lam_init, subln_g)
    y_conf = conformer_conv(ca, cg, conf_dw, conf_ln_g, conf_ln_b)
    y = jnp.concatenate([y_pool, y_sconv, y_attn, y_conf], axis=-1)
    return y @ w_o


def swiglu(x, wg, wu, wd):
    return (jax.nn.silu(x @ wg) * (x @ wu)) @ wd


def moe_swiglu(x, w_router, wg, wu, wd):
    b, s, d = x.shape
    xt = x.reshape(b * s, d)
    logits = (xt @ w_router).astype(jnp.float32)
    top_v, top_i = lax.top_k(logits, TOP_K)
    top_w = jax.nn.softmax(top_v, axis=-1)
    gates = jnp.sum(jax.nn.one_hot(top_i, N_EXPERTS, dtype=jnp.float32) * top_w[..., None], axis=1)
    out = jnp.zeros_like(xt)
    for e in range(N_EXPERTS):
        out = out + gates[:, e:e + 1].astype(x.dtype) * swiglu(xt, wg[e], wu[e], wd[e])
    return out.reshape(b, s, d)


def setup_inputs(seed: int = 0) -> dict:
    key = jax.random.key(seed)
    ks = jax.random.split(key, 32)
    f32 = jnp.float32
    nrm = lambda k, shape, sc: jax.random.normal(k, shape, f32) * sc
    x = jax.random.normal(ks[0], (BATCH, SEQ, D_MODEL), f32)
    offset = jax.random.randint(ks[1], (BATCH, 1), 0, 4096, dtype=jnp.int32)
    positions = (offset + jnp.arange(SEQ, dtype=jnp.int32)[None, :]).astype(jnp.int32)
    return {
        "x": x,
        "positions": positions,
        "w_in": nrm(ks[2], (DEPTH, D_MODEL, IN_COLS), D_MODEL ** -0.5),
        "w_o": nrm(ks[3], (DEPTH, D_MIX, D_MODEL), DN_BETA * D_MIX ** -0.5),
        "pool_w": nrm(ks[4], (DEPTH, N_POOL_GROUPS, POOL_GROUP, POOL_GROUP), POOL_GROUP ** -0.5),
        "pool_scale": 1.0 + nrm(ks[5], (DEPTH, POOL_W), 0.02),
        "sconv_w": nrm(ks[6], (DEPTH, SCONV_K, SCONV_W), SCONV_K ** -0.5),
        "diff_lambda": nrm(ks[7], (DEPTH, 4, DIFF_QK_DIM), 0.1),
        "diff_subln_g": 1.0 + nrm(ks[8], (DEPTH, DIFF_V_DIM), 0.02),
        "conf_dw": nrm(ks[9], (DEPTH, CONF_K, CONF_W), CONF_K ** -0.5),
        "conf_ln_g": 1.0 + nrm(ks[10], (DEPTH, CONF_W), 0.02),
        "conf_ln_b": nrm(ks[11], (DEPTH, CONF_W), 0.02),
        "ln1_g": 1.0 + nrm(ks[12], (DEPTH, D_MODEL), 0.02),
        "ln1_b": nrm(ks[13], (DEPTH, D_MODEL), 0.02),
        "ln2_g": 1.0 + nrm(ks[14], (DEPTH, D_MODEL), 0.02),
        "ln2_b": nrm(ks[15], (DEPTH, D_MODEL), 0.02),
        "ffn_wg": nrm(ks[16], (N_DENSE, D_MODEL, D_FF), D_MODEL ** -0.5),
        "ffn_wu": nrm(ks[17], (N_DENSE, D_MODEL, D_FF), D_MODEL ** -0.5),
        "ffn_wd": nrm(ks[18], (N_DENSE, D_FF, D_MODEL), DN_BETA * D_FF ** -0.5),
        "router_w": nrm(ks[19], (N_MOE, D_MODEL, N_EXPERTS), D_MODEL ** -0.5),
        "moe_wg": nrm(ks[20], (N_MOE, N_EXPERTS, D_MODEL, D_FF_EXPERT), D_MODEL ** -0.5),
        "moe_wu": nrm(ks[21], (N_MOE, N_EXPERTS, D_MODEL, D_FF_EXPERT), D_MODEL ** -0.5),
        "moe_wd": nrm(ks[22], (N_MOE, N_EXPERTS, D_FF_EXPERT, D_MODEL), DN_BETA * D_FF_EXPERT ** -0.5),
    }


def reference(x, positions, w_in, w_o, pool_w, pool_scale, sconv_w, diff_lambda, diff_subln_g,
              conf_dw, conf_ln_g, conf_ln_b, ln1_g, ln1_b, ln2_g, ln2_b,
              ffn_wg, ffn_wu, ffn_wd, router_w, moe_wg, moe_wu, moe_wd):
    cos, sin = rope_tables(positions)
    for l in range(DEPTH):
        lam_init = 0.8 - 0.6 * math.exp(-0.3 * l)
        y = hybrid_mixer(x, w_in[l], w_o[l], pool_w[l], pool_scale[l], sconv_w[l],
                         diff_lambda[l], diff_subln_g[l], conf_dw[l], conf_ln_g[l], conf_ln_b[l],
                         cos, sin, lam_init)
        x = layer_norm(DN_ALPHA * x + y, ln1_g[l], ln1_b[l])
        j = l // 2
        if l % 2 == 0:
            f = swiglu(x, ffn_wg[j], ffn_wu[j], ffn_wd[j])
        else:
            f = moe_swiglu(x, router_w[j], moe_wg[j], moe_wu[j], moe_wd[j])
        x = layer_norm(DN_ALPHA * x + f, ln2_g[l], ln2_b[l])
    return x
```

```python
import functools
import math

import jax
import jax.numpy as jnp
from jax import lax
from jax.experimental import pallas as pl
from jax.experimental.pallas import tpu as pltpu

D_MODEL = 1024
DEPTH = 4
MIX_W = D_MODEL // 4
POOL_WINDOWS = (2, 4, 8, 16)
POOL_GROUP = MIX_W // len(POOL_WINDOWS)
SCONV_K = 3
N_DIFF_HEADS = 4
DIFF_V_DIM = MIX_W // N_DIFF_HEADS
DIFF_QK_DIM = DIFF_V_DIM // 2
ROPE_DIM = DIFF_QK_DIM // 4
ROPE_THETA = 500000.0
CONF_K = 31
N_EXPERTS = 8
DN_ALPHA = (2 * DEPTH) ** 0.25
LN_EPS = 1e-5
IN_COLS = 9 * MIX_W
COL_POOL, COL_SB, COL_SCG, COL_SH, COL_Q, COL_K, COL_V, COL_CA, COL_CG = range(9)

LANES = 128
SUBLANES = 8
VMEM_LIMIT_BYTES = 48 * 1024 * 1024

HALO = 32
CONV_ROWS = 64
ATT_BLOCK = 256


def _tile(n, pref):
    if n <= pref:
        return n
    t = pref - pref % SUBLANES
    while t > SUBLANES and n % t:
        t -= SUBLANES
    assert n % t == 0, (n, pref)
    return t


def _params(*sem):
    return pltpu.CompilerParams(dimension_semantics=sem, vmem_limit_bytes=VMEM_LIMIT_BYTES)


def _layer_norm(v, g, b):
    mu = jnp.mean(v, axis=-1, keepdims=True)
    vc = v - mu
    var = jnp.mean(vc * vc, axis=-1, keepdims=True)
    return vc * lax.rsqrt(var + LN_EPS) * g + b


def _sigmoid(v):
    return 1.0 / (1.0 + jnp.exp(-v))


def _bf16(v):
    return v.astype(jnp.bfloat16)


def _in_proj_kernel(x_ref, w_ref, z_ref):
    xb = _bf16(x_ref[...])
    step = 3 * MIX_W
    for c in range(0, IN_COLS, step):
        z_ref[:, c:c + step] = jnp.dot(xb, w_ref[:, c:c + step], preferred_element_type=jnp.float32)


def _in_proj(x2d, w_bf16):
    t = x2d.shape[0]
    tm = _tile(t, 512)
    return pl.pallas_call(
        _in_proj_kernel,
        grid=(t // tm,),
        in_specs=[pl.BlockSpec((tm, D_MODEL), lambda i: (i, 0)),
                  pl.BlockSpec((D_MODEL, IN_COLS), lambda i: (0, 0))],
        out_specs=pl.BlockSpec((tm, IN_COLS), lambda i: (i, 0)),
        out_shape=jax.ShapeDtypeStruct((t, IN_COLS), jnp.float32),
        compiler_params=_params("arbitrary"),
        name="in_proj",
    )(x2d, w_bf16)


def _rope_table_kernel(pos_ref, freq_ref, c_ref, s1_ref, s2_ref):
    ang = pos_ref[0].astype(jnp.float32) * freq_ref[...]
    d = lax.broadcasted_iota(jnp.int32, ang.shape, 1) & (DIFF_QK_DIM - 1)
    cs = jnp.cos(ang)
    sn = jnp.sin(ang)
    half = ROPE_DIM // 2
    c_ref[0] = jnp.where(d < ROPE_DIM, cs, 1.0)
    s1_ref[0] = jnp.where(d < half, -sn, 0.0)
    s2_ref[0] = jnp.where((d >= half) & (d < ROPE_DIM), sn, 0.0)


def _rope_tables(positions):
    b, s = positions.shape
    ts = _tile(s, 512)
    inv = ROPE_THETA ** (-jnp.arange(0, ROPE_DIM, 2, dtype=jnp.float32) / ROPE_DIM)
    d = jnp.arange(LANES) % DIFF_QK_DIM
    freq = jnp.where(d < ROPE_DIM, inv[d % (ROPE_DIM // 2)], 0.0).astype(jnp.float32)[None, :]
    tab = jax.ShapeDtypeStruct((b, s, LANES), jnp.float32)
    spec = pl.BlockSpec((1, ts, LANES), lambda bi, i: (bi, i, 0))
    return pl.pallas_call(
        _rope_table_kernel,
        grid=(b, s // ts),
        in_specs=[pl.BlockSpec((1, ts, 1), lambda bi, i: (bi, i, 0)),
                  pl.BlockSpec((1, LANES), lambda bi, i: (0, 0))],
        out_specs=[spec, spec, spec],
        out_shape=[tab, tab, tab],
        compiler_params=_params("arbitrary", "arbitrary"),
        name="rope_tables",
    )(positions[:, :, None], freq)


def _causal_taps(buf_ref, w_ref, n_taps, r0):
    acc = None
    for j in range(n_taps):
        start = HALO + r0 - (n_taps - 1) + j
        term = w_ref[j:j + 1, :] * buf_ref[start:start + CONV_ROWS, :]
        acc = term if acc is None else acc + term
    return acc


def _mixer_kernel(p_ref, sb_ref, scg_ref, sh_ref, q_ref, k_ref, v_ref, ca_ref, cg_ref,
                  p_halo, scg_halo, sh_halo, ca_halo, cg_halo,
                  c_ref, s1_ref, s2_ref,
                  poolw_ref, poolscale_ref, pooltap_ref, poolwin_ref,
                  sconvw_ref, confw_ref, confg_ref, confb_ref,
                  ypool_ref, ysconv_ref, yconf_ref, qo_ref, ko_ref, vt_ref,
                  pbuf, gbuf, ubuf, dbuf):
    i = pl.program_id(1)
    ts = p_ref.shape[1]
    keep = jnp.where(i == 0, 0.0, 1.0)

    pbuf[0:HALO, :] = p_halo[0] * keep
    pbuf[HALO:, :] = p_ref[0]
    gbuf[0:HALO, :] = scg_halo[0] * sh_halo[0] * keep
    gbuf[HALO:, :] = scg_ref[0] * sh_ref[0]
    ubuf[0:HALO, :] = ca_halo[0] * _sigmoid(cg_halo[0]) * keep
    ubuf[HALO:, :] = ca_ref[0] * _sigmoid(cg_ref[0])

    n_pool_taps = max(POOL_WINDOWS)
    for r0 in range(0, ts, CONV_ROWS):
        rows = slice(r0, r0 + CONV_ROWS)
        wsum = _causal_taps(pbuf, pooltap_ref, n_pool_taps, r0)
        tpos = (i * ts + r0 + 1 + lax.broadcasted_iota(jnp.int32, (CONV_ROWS, MIX_W), 0)).astype(jnp.float32)
        cnt = jnp.minimum(tpos, poolwin_ref[...])
        dbuf[rows, :] = wsum / cnt - pbuf[HALO + r0:HALO + r0 + CONV_ROWS, :]
        sc = _causal_taps(gbuf, sconvw_ref, SCONV_K, r0)
        ysconv_ref[0, rows, :] = _bf16(sb_ref[0, rows, :] * sc)
        u = _causal_taps(ubuf, confw_ref, CONF_K, r0)
        u = _layer_norm(u, confg_ref[...], confb_ref[...])
        yconf_ref[0, rows, :] = _bf16(u * _sigmoid(u))

    y = jnp.dot(_bf16(dbuf[...]), poolw_ref[...], preferred_element_type=jnp.float32)
    ypool_ref[0] = _bf16(y * poolscale_ref[...])

    c, s1, s2 = c_ref[0], s1_ref[0], s2_ref[0]
    half = ROPE_DIM // 2
    for src, dst, scale in ((q_ref, qo_ref, DIFF_QK_DIM ** -0.5), (k_ref, ko_ref, None)):
        for lo in range(0, MIX_W, LANES):
            xh = src[0, :, lo:lo + LANES]
            r = xh * c + pltpu.roll(xh, LANES - half, 1) * s1 + pltpu.roll(xh, half, 1) * s2
            if scale is not None:
                r = r * scale
            dst[0, :, lo:lo + LANES] = _bf16(r)
    vt_ref[0, 0] = _bf16(v_ref[0].T)


def _mixers(z3, tabs, poolw_bd, pool_scale, pool_tap, pool_win, sconv_w, conf_dw, conf_g, conf_b):
    b, s, _ = z3.shape
    ts = ATT_BLOCK if s % ATT_BLOCK == 0 else s
    nblk = s // ts
    per_halo = ts // HALO

    def col(cidx):
        return pl.BlockSpec((1, ts, MIX_W), lambda bi, i: (bi, i, cidx))

    def halo(cidx):
        return pl.BlockSpec((1, HALO, MIX_W), lambda bi, i: (bi, jnp.maximum(i * per_halo - 1, 0), cidx))

    def const(shape):
        return pl.BlockSpec(shape, lambda bi, i: (0,) * len(shape))

    tab_spec = pl.BlockSpec((1, ts, LANES), lambda bi, i: (bi, i, 0))
    y_spec = pl.BlockSpec((1, ts, MIX_W), lambda bi, i: (bi, i, 0))
    y_shape = jax.ShapeDtypeStruct((b, s, MIX_W), jnp.bfloat16)
    return pl.pallas_call(
        _mixer_kernel,
        grid=(b, nblk),
        in_specs=[col(c) for c in range(9)]
        + [halo(c) for c in (COL_POOL, COL_SCG, COL_SH, COL_CA, COL_CG)]
        + [tab_spec, tab_spec, tab_spec]
        + [const((MIX_W, MIX_W)), const((1, MIX_W)), const((max(POOL_WINDOWS), MIX_W)), const((1, MIX_W)),
           const((SCONV_K, MIX_W)), const((CONF_K, MIX_W)), const((1, MIX_W)), const((1, MIX_W))],
        out_specs=[y_spec, y_spec, y_spec, y_spec, y_spec,
                   pl.BlockSpec((1, 1, MIX_W, ts), lambda bi, i: (bi, i, 0, 0))],
        out_shape=[y_shape, y_shape, y_shape, y_shape, y_shape,
                   jax.ShapeDtypeStruct((b, nblk, MIX_W, ts), jnp.bfloat16)],
        scratch_shapes=[pltpu.VMEM((HALO + ts, MIX_W), jnp.float32)] * 3 + [pltpu.VMEM((ts, MIX_W), jnp.float32)],
        compiler_params=_params("arbitrary", "arbitrary"),
        name="mixers",
    )(*([z3] * 14), *tabs, poolw_bd, pool_scale, pool_tap, pool_win, sconv_w, conf_dw, conf_g, conf_b)


def _attn_kernel(lam_ref, g_ref, q_ref, k_ref, vt_ref, o_ref, m_sc, l_sc, acc_sc, *, out_scale, lam_init):
    qi = pl.program_id(2)
    tq = q_ref.shape[1]
    q = q_ref[0]
    lane = lax.broadcasted_iota(jnp.int32, q.shape, 1)
    qz = []
    for hm in range(4):
        lo = hm * DIFF_QK_DIM
        qz.append(jnp.where((lane >= lo) & (lane < lo + DIFF_QK_DIM), q, jnp.zeros_like(q)))

    m_sc[...] = jnp.full(m_sc.shape, -jnp.inf, jnp.float32)
    l_sc[...] = jnp.zeros(l_sc.shape, jnp.float32)
    acc_sc[...] = jnp.zeros(acc_sc.shape, jnp.float32)

    def step(kb, diagonal):
        kt = k_ref[0, pl.ds(pl.multiple_of(kb * tq, tq), tq), :]
        vt = vt_ref[0, kb]
        for hm in range(4):
            s = lax.dot_general(kt, qz[hm], (((1,), (1,)), ((), ())),
                                preferred_element_type=jnp.float32)
            if diagonal:
                kr = lax.broadcasted_iota(jnp.int32, s.shape, 0)
                qc = lax.broadcasted_iota(jnp.int32, s.shape, 1)
                s = jnp.where(kr <= qc, s, -jnp.inf)
            m_old = m_sc[hm]
            m_new = jnp.maximum(m_old, jnp.max(s, axis=0, keepdims=True))
            a = jnp.exp(m_old - m_new)
            p = jnp.exp(s - m_new)
            l_sc[hm] = l_sc[hm] * a + jnp.sum(p, axis=0, keepdims=True)
            h = hm // 2
            pv = jnp.dot(vt[h * DIFF_V_DIM:(h + 1) * DIFF_V_DIM, :], _bf16(p),
                         preferred_element_type=jnp.float32)
            acc_sc[hm] = acc_sc[hm] * a + pv
            m_sc[hm] = m_new

    def body(kb, carry):
        step(kb, False)
        return carry

    lax.fori_loop(0, qi, body, 0)
    step(qi, True)

    lp = lam_ref[...]
    lam = (jnp.exp(jnp.sum(lp[0:1] * lp[1:2], axis=1, keepdims=True))
           - jnp.exp(jnp.sum(lp[2:3] * lp[3:4], axis=1, keepdims=True)) + lam_init)
    outs = []
    for h in range(2):
        o = acc_sc[2 * h] / l_sc[2 * h] - lam * (acc_sc[2 * h + 1] / l_sc[2 * h + 1])
        ms = jnp.mean(o * o, axis=0, keepdims=True)
        outs.append(o * lax.rsqrt(ms + LN_EPS) * g_ref[...] * out_scale)
    o_ref[0] = _bf16(jnp.concatenate(outs, axis=0).T)


def _attention(q_rot, k_rot, vt, lam_params, subln_g, lam_init):
    b, s, _ = q_rot.shape
    nblk, tq = vt.shape[1], vt.shape[3]
    pairs = MIX_W // LANES
    kern = functools.partial(_attn_kernel, out_scale=1.0 - lam_init, lam_init=lam_init)
    return pl.pallas_call(
        kern,
        grid=(b, pairs, nblk),
        in_specs=[pl.BlockSpec((4, DIFF_QK_DIM), lambda bi, hp, qi: (0, 0)),
                  pl.BlockSpec((DIFF_V_DIM, 1), lambda bi, hp, qi: (0, 0)),
                  pl.BlockSpec((1, tq, LANES), lambda bi, hp, qi: (bi, qi, hp)),
                  pl.BlockSpec((1, s, LANES), lambda bi, hp, qi: (bi, 0, hp)),
                  pl.BlockSpec((1, nblk, LANES, tq), lambda bi, hp, qi: (bi, 0, hp, 0))],
        out_specs=pl.BlockSpec((1, tq, LANES), lambda bi, hp, qi: (bi, qi, hp)),
        out_shape=jax.ShapeDtypeStruct((b, s, MIX_W), jnp.bfloat16),
        scratch_shapes=[pltpu.VMEM((4, 1, tq), jnp.float32), pltpu.VMEM((4, 1, tq), jnp.float32),
                        pltpu.VMEM((4, DIFF_V_DIM, tq), jnp.float32)],
        compiler_params=_params("arbitrary", "arbitrary", "arbitrary"),
        name="diff_attention",
    )(lam_params, subln_g[:, None], q_rot, k_rot, vt)


def _out_proj_kernel(yp_ref, ys_ref, ya_ref, yc_ref, x_ref, w_ref, g_ref, b_ref, o_ref):
    acc = None
    for gi, y_ref in enumerate((yp_ref, ys_ref, ya_ref, yc_ref)):
        part = jnp.dot(y_ref[...], w_ref[gi * MIX_W:(gi + 1) * MIX_W, :], preferred_element_type=jnp.float32)
        acc = part if acc is None else acc + part
    o_ref[...] = _layer_norm(DN_ALPHA * x_ref[...] + acc, g_ref[...], b_ref[...])


def _out_proj(ys, x2d, w_bf16, g, b):
    t = x2d.shape[0]
    tm = _tile(t, 512)
    y_spec = pl.BlockSpec((tm, MIX_W), lambda i: (i, 0))
    row = pl.BlockSpec((tm, D_MODEL), lambda i: (i, 0))
    vec = pl.BlockSpec((1, D_MODEL), lambda i: (0, 0))
    return pl.pallas_call(
        _out_proj_kernel,
        grid=(t // tm,),
        in_specs=[y_spec] * 4 + [row, pl.BlockSpec((D_MODEL, D_MODEL), lambda i: (0, 0)), vec, vec],
        out_specs=row,
        out_shape=jax.ShapeDtypeStruct((t, D_MODEL), jnp.float32),
        compiler_params=_params("arbitrary"),
        name="out_proj_ln",
    )(*ys, x2d, w_bf16, g, b)


def _ffn_kernel(x_ref, wg_ref, wu_ref, wd_ref, g_ref, b_ref, o_ref, acc_ref):
    f = pl.program_id(1)
    xb = _bf16(x_ref[...])
    gate = jnp.dot(xb, wg_ref[...], preferred_element_type=jnp.float32)
    up = jnp.dot(xb, wu_ref[...], preferred_element_type=jnp.float32)
    part = jnp.dot(_bf16(gate * _sigmoid(gate) * up), wd_ref[...], preferred_element_type=jnp.float32)

    @pl.when(f == 0)
    def _():
        acc_ref[...] = part

    @pl.when(f > 0)
    def _():
        acc_ref[...] += part

    @pl.when(f == pl.num_programs(1) - 1)
    def _():
        o_ref[...] = _layer_norm(DN_ALPHA * x_ref[...] + acc_ref[...], g_ref[...], b_ref[...])


def _ffn(x2d, wg, wu, wd, g, b):
    t = x2d.shape[0]
    d_ff = wg.shape[1]
    tm = _tile(t, 512)
    tf = d_ff // 2 if (d_ff // 2) % LANES == 0 else d_ff
    row = pl.BlockSpec((tm, D_MODEL), lambda i, f: (i, 0))
    vec = pl.BlockSpec((1, D_MODEL), lambda i, f: (0, 0))
    return pl.pallas_call(
        _ffn_kernel,
        grid=(t // tm, d_ff // tf),
        in_specs=[row,
                  pl.BlockSpec((D_MODEL, tf), lambda i, f: (0, f)),
                  pl.BlockSpec((D_MODEL, tf), lambda i, f: (0, f)),
                  pl.BlockSpec((tf, D_MODEL), lambda i, f: (f, 0)),
                  vec, vec],
        out_specs=row,
        out_shape=jax.ShapeDtypeStruct((t, D_MODEL), jnp.float32),
        scratch_shapes=[pltpu.VMEM((tm, D_MODEL), jnp.float32)],
        compiler_params=_params("arbitrary", "arbitrary"),
        name="ffn_ln",
    )(x2d, wg, wu, wd, g, b)


def _router_kernel(x_ref, wr_ref, idx_ref, gate_ref, cnt_ref, carry_ref):
    @pl.when(pl.program_id(0) == 0)
    def _():
        carry_ref[...] = jnp.zeros_like(carry_ref)

    tr = x_ref.shape[0]
    logits = jnp.dot(x_ref[...], wr_ref[...], preferred_element_type=jnp.float32,
                     precision=lax.Precision.HIGHEST)
    lane = lax.broadcasted_iota(jnp.int32, logits.shape, 1).astype(jnp.float32)
    lg = jnp.where(lane < N_EXPERTS, logits, -jnp.inf)
    m1 = jnp.max(lg, axis=1, keepdims=True)
    i1 = jnp.min(jnp.where(lg == m1, lane, float(LANES)), axis=1, keepdims=True)
    lg2 = jnp.where(lane == i1, -jnp.inf, lg)
    m2 = jnp.max(lg2, axis=1, keepdims=True)
    i2 = jnp.min(jnp.where(lg2 == m2, lane, float(LANES)), axis=1, keepdims=True)
    e2 = jnp.exp(m2 - m1)
    den = 1.0 + e2
    w1 = 1.0 / den
    w2 = e2 / den

    sel1 = lane == i1
    sel2 = lane == i2
    onehot = jnp.where(sel1 | sel2, 1.0, 0.0)
    rr = lax.broadcasted_iota(jnp.int32, (tr, tr), 0)
    cc = lax.broadcasted_iota(jnp.int32, (tr, tr), 1)
    lower = _bf16(jnp.where(cc < rr, 1.0, 0.0))
    carry = carry_ref[0:1, :]
    before = jnp.dot(lower, _bf16(onehot), preferred_element_type=jnp.float32) + carry
    r1 = jnp.sum(jnp.where(sel1, before, 0.0), axis=1, keepdims=True)
    r2 = jnp.sum(jnp.where(sel2, before, 0.0), axis=1, keepdims=True)
    carry_ref[...] = jnp.broadcast_to(carry + jnp.sum(onehot, axis=0, keepdims=True), carry_ref.shape)
    cnt_ref[...] = carry_ref[...]
    info = jnp.where(lane == 0.0, i1, jnp.where(lane == 1.0, i2,
                     jnp.where(lane == 2.0, r1, jnp.where(lane == 3.0, r2, 0.0))))
    idx_ref[...] = info.astype(jnp.int32)
    gate_ref[...] = jnp.where(lane == 0.0, w1, jnp.where(lane == 1.0, w2, 0.0))


def _router(x2d, wr_pad):
    t = x2d.shape[0]
    tr = _tile(t, 512)
    wide = pl.BlockSpec((tr, LANES), lambda i: (i, 0))
    return pl.pallas_call(
        _router_kernel,
        grid=(t // tr,),
        in_specs=[pl.BlockSpec((tr, D_MODEL), lambda i: (i, 0)),
                  pl.BlockSpec((D_MODEL, LANES), lambda i: (0, 0))],
        out_specs=[wide, wide, pl.BlockSpec((SUBLANES, LANES), lambda i: (0, 0))],
        out_shape=[jax.ShapeDtypeStruct((t, LANES), jnp.int32),
                   jax.ShapeDtypeStruct((t, LANES), jnp.float32),
                   jax.ShapeDtypeStruct((SUBLANES, LANES), jnp.float32)],
        scratch_shapes=[pltpu.VMEM((SUBLANES, LANES), jnp.float32)],
        compiler_params=_params("arbitrary"),
        name="router_rank",
    )(x2d, wr_pad)


def _dispatch_kernel(ends_ref, pos_ref, x_ref, xs_ref, zero_ref, sem, zsem, *, tm):
    td = x_ref.shape[0]

    @pl.when(pl.program_id(0) == 0)
    def _():
        zero_ref[...] = jnp.zeros_like(zero_ref)

        def tile_copy(start):
            return pltpu.make_async_copy(zero_ref, xs_ref.at[pl.ds(pl.multiple_of(start, tm), tm), :], zsem)

        def tail(fn):
            def body(j, carry):
                fn(tile_copy(j * tm))
                return carry
            lax.fori_loop(ends_ref[N_EXPERTS + 1], xs_ref.shape[0] // tm, body, 0)

        for e in range(N_EXPERTS):
            @pl.when(ends_ref[e + 1] > ends_ref[e])
            def _():
                tile_copy(ends_ref[e + 1] - tm).start()
        tail(lambda cp: cp.start())
        for e in range(N_EXPERTS):
            @pl.when(ends_ref[e + 1] > ends_ref[e])
            def _():
                tile_copy(ends_ref[e + 1] - tm).wait()
        tail(lambda cp: cp.wait())

    def row_copy(r, k):
        p = pos_ref[0, 0, 2 * r + k]
        return pltpu.make_async_copy(x_ref.at[pl.ds(r, 1), :], xs_ref.at[pl.ds(p, 1), :], sem)

    def start(r, carry):
        row_copy(r, 0).start()
        row_copy(r, 1).start()
        return carry

    def wait(r, carry):
        row_copy(r, 0).wait()
        row_copy(r, 1).wait()
        return carry

    lax.fori_loop(0, td, start, 0)
    lax.fori_loop(0, td, wait, 0)


def _dispatch(x2d, pos, ends0, n_rows, tm):
    t = x2d.shape[0]
    td = _tile(t, 256)
    grid_spec = pltpu.PrefetchScalarGridSpec(
        num_scalar_prefetch=1,
        grid=(t // td,),
        in_specs=[pl.BlockSpec((1, 1, 2 * td), lambda i, ends: (i, 0, 0), memory_space=pltpu.SMEM),
                  pl.BlockSpec((td, D_MODEL), lambda i, ends: (i, 0))],
        out_specs=pl.BlockSpec(memory_space=pl.ANY),
        scratch_shapes=[pltpu.VMEM((tm, D_MODEL), jnp.float32),
                        pltpu.SemaphoreType.DMA(()), pltpu.SemaphoreType.DMA(())],
    )
    return pl.pallas_call(
        functools.partial(_dispatch_kernel, tm=tm),
        grid_spec=grid_spec,
        out_shape=jax.ShapeDtypeStruct((n_rows, D_MODEL), jnp.float32),
        compiler_params=_params("arbitrary"),
        name="moe_dispatch",
    )(ends0, pos.reshape(t // td, 1, 2 * td), x2d)


def _moe_ffn_kernel(te_ref, nu_ref, xs_ref, wg_ref, wu_ref, wd_ref, ys_ref):
    i = pl.program_id(0)
    f = pl.program_id(1)

    @pl.when(i < nu_ref[0])
    def _():
        xb = _bf16(xs_ref[...])
        gate = jnp.dot(xb, wg_ref[0], preferred_element_type=jnp.float32)
        up = jnp.dot(xb, wu_ref[0], preferred_element_type=jnp.float32)
        part = jnp.dot(_bf16(gate * _sigmoid(gate) * up), wd_ref[0], preferred_element_type=jnp.float32)

        @pl.when(f == 0)
        def _():
            ys_ref[...] = part

        @pl.when(f > 0)
        def _():
            ys_ref[...] += part

    @pl.when(i >= nu_ref[0])
    def _():
        ys_ref[...] = jnp.zeros_like(ys_ref)


def _moe_ffn(xs, wg, wu, wd, tile_expert, n_used, tm):
    n_rows = xs.shape[0]
    n_tiles = n_rows // tm
    d_ff = wg.shape[2]
    tf = _tile(d_ff, 512)
    nf = d_ff // tf

    def row_idx(i, f, te, nu):
        return (jnp.minimum(i, nu[0] - 1), 0)

    def out_idx(i, f, te, nu):
        return (i, 0)

    def f_idx(i, f, nu):
        return jnp.where(i < nu[0], f, nf - 1)

    def up_idx(i, f, te, nu):
        return (te[jnp.minimum(i, nu[0] - 1)], 0, f_idx(i, f, nu))

    def down_idx(i, f, te, nu):
        return (te[jnp.minimum(i, nu[0] - 1)], f_idx(i, f, nu), 0)

    grid_spec = pltpu.PrefetchScalarGridSpec(
        num_scalar_prefetch=2,
        grid=(n_tiles, nf),
        in_specs=[pl.BlockSpec((tm, D_MODEL), row_idx),
                  pl.BlockSpec((1, D_MODEL, tf), up_idx),
                  pl.BlockSpec((1, D_MODEL, tf), up_idx),
                  pl.BlockSpec((1, tf, D_MODEL), down_idx)],
        out_specs=pl.BlockSpec((tm, D_MODEL), out_idx),
    )
    return pl.pallas_call(
        _moe_ffn_kernel,
        grid_spec=grid_spec,
        out_shape=jax.ShapeDtypeStruct((n_rows, D_MODEL), jnp.float32),
        compiler_params=_params("arbitrary", "arbitrary"),
        name="moe_grouped_ffn",
    )(tile_expert, n_used, xs, wg, wu, wd)


def _combine_kernel(pos_ref, gate_ref, x_ref, ys_ref, g_ref, b_ref, o_ref, buf, sem):
    tc = x_ref.shape[0]

    def row_copy(r, k):
        p = pos_ref[0, 0, 2 * r + k]
        return pltpu.make_async_copy(ys_ref.at[pl.ds(p, 1), :], buf.at[k, pl.ds(r, 1), :], sem)

    def start(r, carry):
        row_copy(r, 0).start()
        row_copy(r, 1).start()
        return carry

    def wait(r, carry):
        row_copy(r, 0).wait()
        row_copy(r, 1).wait()
        return carry

    lax.fori_loop(0, tc, start, 0)
    lax.fori_loop(0, tc, wait, 0)
    gates = gate_ref[...]
    f = gates[:, 0:1] * buf[0] + gates[:, 1:2] * buf[1]
    o_ref[...] = _layer_norm(DN_ALPHA * x_ref[...] + f, g_ref[...], b_ref[...])


def _combine(x2d, pos, gates, ys, g, b):
    t = x2d.shape[0]
    tc = _tile(t, 256)
    row = pl.BlockSpec((tc, D_MODEL), lambda i: (i, 0))
    vec = pl.BlockSpec((1, D_MODEL), lambda i: (0, 0))
    return pl.pallas_call(
        _combine_kernel,
        grid=(t // tc,),
        in_specs=[pl.BlockSpec((1, 1, 2 * tc), lambda i: (i, 0, 0), memory_space=pltpu.SMEM),
                  pl.BlockSpec((tc, LANES), lambda i: (i, 0)),
                  row,
                  pl.BlockSpec(memory_space=pl.ANY),
                  vec, vec],
        out_specs=row,
        out_shape=jax.ShapeDtypeStruct((t, D_MODEL), jnp.float32),
        scratch_shapes=[pltpu.VMEM((2, tc, D_MODEL), jnp.float32), pltpu.SemaphoreType.DMA(())],
        compiler_params=_params("arbitrary"),
        name="moe_combine_ln",
    )(pos.reshape(t // tc, 1, 2 * tc), gates, x2d, ys, g, b)


def _moe(x2d, wr, wg, wu, wd, g, b):
    t = x2d.shape[0]
    tm = 1024 if t >= 8192 else 128
    wr_pad = jnp.zeros((D_MODEL, LANES), jnp.float32).at[:, :N_EXPERTS].set(wr)
    idx, gates, cnt = _router(x2d, wr_pad)
    counts = cnt[0, :N_EXPERTS].astype(jnp.int32)
    padded = (counts + tm - 1) // tm * tm
    ends = jnp.cumsum(padded)
    starts = ends - padded
    pos = (starts[idx[:, 0:2]] + idx[:, 2:4]).astype(jnp.int32)
    n_tiles = (2 * t) // tm + N_EXPERTS
    tile_start = jnp.arange(n_tiles, dtype=jnp.int32) * tm
    tile_expert = jnp.minimum(jnp.sum(tile_start[:, None] >= ends[None, :], axis=1), N_EXPERTS - 1).astype(jnp.int32)
    n_used = (ends[-1:] // tm).astype(jnp.int32)
    ends0 = jnp.concatenate([jnp.zeros((1,), jnp.int32), ends.astype(jnp.int32), n_used])
    xs = _dispatch(x2d, pos, ends0, n_tiles * tm, tm)
    ys = _moe_ffn(xs, wg, wu, wd, tile_expert, n_used, tm)
    return _combine(x2d, pos, gates, ys, g, b)


def _pool_constants(pool_w):
    n_g = len(POOL_WINDOWS)
    bd = jnp.zeros((MIX_W, MIX_W), jnp.float32)
    for gi in range(n_g):
        sl = slice(gi * POOL_GROUP, (gi + 1) * POOL_GROUP)
        bd = bd.at[sl, sl].set(pool_w[gi])
    win = jnp.repeat(jnp.asarray(POOL_WINDOWS, jnp.float32), POOL_GROUP)[None, :]
    lag = (max(POOL_WINDOWS) - 1 - jnp.arange(max(POOL_WINDOWS), dtype=jnp.float32))[:, None]
    tap = jnp.where(lag < win, 1.0, 0.0).astype(jnp.float32)
    return _bf16(bd), tap, win


def kernel(x, positions, w_in, w_o, pool_w, pool_scale, sconv_w, diff_lambda, diff_subln_g,
           conf_dw, conf_ln_g, conf_ln_b, ln1_g, ln1_b, ln2_g, ln2_b,
           ffn_wg, ffn_wu, ffn_wd, router_w, moe_wg, moe_wu, moe_wd):
    b, s, d = x.shape
    assert d == D_MODEL and s % HALO == 0
    t = b * s
    tabs = _rope_tables(positions)
    x2d = x.reshape(t, d)
    for l in range(DEPTH):
        lam_init = 0.8 - 0.6 * math.exp(-0.3 * l)
        z = _in_proj(x2d, _bf16(w_in[l]))
        poolw_bd, pool_tap, pool_win = _pool_constants(pool_w[l])
        y_pool, y_sconv, y_conf, q_rot, k_rot, vt = _mixers(
            z.reshape(b, s, IN_COLS), tabs, poolw_bd, pool_scale[l][None, :], pool_tap, pool_win,
            sconv_w[l], conf_dw[l], conf_ln_g[l][None, :], conf_ln_b[l][None, :])
        y_attn = _attention(q_rot, k_rot, vt, diff_lambda[l], diff_subln_g[l], lam_init)
        ys = [y.reshape(t, MIX_W) for y in (y_pool, y_sconv, y_attn, y_conf)]
        x2d = _out_proj(ys, x2d, _bf16(w_o[l]), ln1_g[l][None, :], ln1_b[l][None, :])
        j = l // 2
        if l % 2 == 0:
            x2d = _ffn(x2d, _bf16(ffn_wg[j]), _bf16(ffn_wu[j]), _bf16(ffn_wd[j]),
                       ln2_g[l][None, :], ln2_b[l][None, :])
        else:
            x2d = _moe(x2d, router_w[j], _bf16(moe_wg[j]), _bf16(moe_wu[j]), _bf16(moe_wd[j]),
                       ln2_g[l][None, :], ln2_b[l][None, :])
    return x2d.reshape(b, s, d)
```

```python
import functools
import math

import jax
import jax.numpy as jnp
from jax import lax
from jax.experimental import pallas as pl
from jax.experimental.pallas import tpu as pltpu

D_MODEL = 1024
DEPTH = 4
MIX_W = D_MODEL // 4
POOL_WINDOWS = (2, 4, 8, 16)
POOL_GROUP = MIX_W // len(POOL_WINDOWS)
SCONV_K = 3
N_DIFF_HEADS = 4
DIFF_V_DIM = MIX_W // N_DIFF_HEADS
DIFF_QK_DIM = DIFF_V_DIM // 2
ROPE_DIM = DIFF_QK_DIM // 4
ROPE_THETA = 500000.0
CONF_K = 31
N_EXPERTS = 8
DN_ALPHA = (2 * DEPTH) ** 0.25
LN_EPS = 1e-5
IN_COLS = 9 * MIX_W
COL_POOL, COL_SB, COL_SCG, COL_SH, COL_Q, COL_K, COL_V, COL_CA, COL_CG = range(9)

LANES = 128
SUBLANES = 8
VMEM_LIMIT_BYTES = 48 * 1024 * 1024

HALO = 32
CONV_ROWS = 64
ATT_BLOCK = 256
ATT_CHUNK = 64
BF16_ROWS = 2 * SUBLANES
V_ROWS = DIFF_V_DIM + BF16_ROWS


def _tile(n, pref):
    if n <= pref:
        return n
    t = pref - pref % SUBLANES
    while t > SUBLANES and n % t:
        t -= SUBLANES
    assert n % t == 0, (n, pref)
    return t


def _params(*sem):
    return pltpu.CompilerParams(dimension_semantics=sem, vmem_limit_bytes=VMEM_LIMIT_BYTES)


def _layer_norm(v, g, b):
    mu = jnp.mean(v, axis=-1, keepdims=True)
    vc = v - mu
    var = jnp.mean(vc * vc, axis=-1, keepdims=True)
    return vc * lax.rsqrt(var + LN_EPS) * g + b


def _sigmoid(v):
    return 1.0 / (1.0 + jnp.exp(-v))


def _bf16(v):
    return v.astype(jnp.bfloat16)


def _in_proj_kernel(x_ref, w_ref, z_ref):
    xb = _bf16(x_ref[...])
    step = 3 * MIX_W
    for c in range(0, IN_COLS, step):
        z_ref[:, c:c + step] = jnp.dot(xb, w_ref[:, c:c + step], preferred_element_type=jnp.float32)


def _in_proj(x2d, w_bf16):
    t = x2d.shape[0]
    tm = _tile(t, 512)
    return pl.pallas_call(
        _in_proj_kernel,
        grid=(t // tm,),
        in_specs=[pl.BlockSpec((tm, D_MODEL), lambda i: (i, 0)),
                  pl.BlockSpec((D_MODEL, IN_COLS), lambda i: (0, 0))],
        out_specs=pl.BlockSpec((tm, IN_COLS), lambda i: (i, 0)),
        out_shape=jax.ShapeDtypeStruct((t, IN_COLS), jnp.float32),
        compiler_params=_params("arbitrary"),
        name="in_proj",
    )(x2d, w_bf16)


def _rope_table_kernel(pos_ref, freq_ref, c_ref, s1_ref, s2_ref):
    ang = pos_ref[0].astype(jnp.float32) * freq_ref[...]
    d = lax.broadcasted_iota(jnp.int32, ang.shape, 1) & (DIFF_QK_DIM - 1)
    cs = jnp.cos(ang)
    sn = jnp.sin(ang)
    half = ROPE_DIM // 2
    c_ref[0] = jnp.where(d < ROPE_DIM, cs, 1.0)
    s1_ref[0] = jnp.where(d < half, -sn, 0.0)
    s2_ref[0] = jnp.where((d >= half) & (d < ROPE_DIM), sn, 0.0)


def _rope_tables(positions):
    b, s = positions.shape
    ts = _tile(s, 512)
    inv = ROPE_THETA ** (-jnp.arange(0, ROPE_DIM, 2, dtype=jnp.float32) / ROPE_DIM)
    d = jnp.arange(LANES) % DIFF_QK_DIM
    freq = jnp.where(d < ROPE_DIM, inv[d % (ROPE_DIM // 2)], 0.0).astype(jnp.float32)[None, :]
    tab = jax.ShapeDtypeStruct((b, s, LANES), jnp.float32)
    spec = pl.BlockSpec((1, ts, LANES), lambda bi, i: (bi, i, 0))
    return pl.pallas_call(
        _rope_table_kernel,
        grid=(b, s // ts),
        in_specs=[pl.BlockSpec((1, ts, 1), lambda bi, i: (bi, i, 0)),
                  pl.BlockSpec((1, LANES), lambda bi, i: (0, 0))],
        out_specs=[spec, spec, spec],
        out_shape=[tab, tab, tab],
        compiler_params=_params("arbitrary", "arbitrary"),
        name="rope_tables",
    )(positions[:, :, None], freq)


def _fill_shifted(buf_ref, residues):
    n = buf_ref.shape[1] - SUBLANES
    for r in residues:
        buf_ref[r, 0:n, :] = buf_ref[0, r:r + n, :]


def _tap_residues(n_taps):
    return sorted({(j - (n_taps - 1)) % SUBLANES for j in range(n_taps)} - {0})


def _causal_taps(buf_ref, w_ref, n_taps, r0):
    acc = None
    for j in range(n_taps):
        start = HALO + r0 - (n_taps - 1) + j
        r = start % SUBLANES
        term = w_ref[j:j + 1, :] * buf_ref[r, start - r:start - r + CONV_ROWS, :]
        acc = term if acc is None else acc + term
    return acc


def _mixer_kernel(p_ref, sb_ref, scg_ref, sh_ref, q_ref, k_ref, v_ref, ca_ref, cg_ref,
                  p_halo, scg_halo, sh_halo, ca_halo, cg_halo,
                  c_ref, s1_ref, s2_ref,
                  poolw_ref, poolscale_ref, pooltap_ref, poolwin_ref,
                  sconvw_ref, confw_ref, confg_ref, confb_ref,
                  ypool_ref, ysconv_ref, yconf_ref, qo_ref, ko_ref, vt_ref,
                  pbuf, gbuf, ubuf, dbuf):
    i = pl.program_id(1)
    ts = p_ref.shape[1]
    keep = jnp.where(i == 0, 0.0, 1.0)

    pbuf[0, 0:HALO, :] = p_halo[0] * keep
    pbuf[0, HALO:, :] = p_ref[0]
    gbuf[0, 0:HALO, :] = scg_halo[0] * sh_halo[0] * keep
    gbuf[0, HALO:, :] = scg_ref[0] * sh_ref[0]
    ubuf[0, 0:HALO, :] = ca_halo[0] * _sigmoid(cg_halo[0]) * keep
    ubuf[0, HALO:, :] = ca_ref[0] * _sigmoid(cg_ref[0])
    n_pool_taps = max(POOL_WINDOWS)
    _fill_shifted(pbuf, _tap_residues(n_pool_taps))
    _fill_shifted(gbuf, _tap_residues(SCONV_K))
    _fill_shifted(ubuf, _tap_residues(CONF_K))

    for r0 in range(0, ts, CONV_ROWS):
        rows = slice(r0, r0 + CONV_ROWS)
        wsum = _causal_taps(pbuf, pooltap_ref, n_pool_taps, r0)
        tpos = (i * ts + r0 + 1 + lax.broadcasted_iota(jnp.int32, (CONV_ROWS, MIX_W), 0)).astype(jnp.float32)
        cnt = jnp.minimum(tpos, poolwin_ref[...])
        dbuf[rows, :] = wsum / cnt - pbuf[0, HALO + r0:HALO + r0 + CONV_ROWS, :]
        sc = _causal_taps(gbuf, sconvw_ref, SCONV_K, r0)
        ysconv_ref[0, rows, :] = _bf16(sb_ref[0, rows, :] * sc)
        u = _causal_taps(ubuf, confw_ref, CONF_K, r0)
        u = _layer_norm(u, confg_ref[...], confb_ref[...])
        yconf_ref[0, rows, :] = _bf16(u * _sigmoid(u))

    y = jnp.dot(_bf16(dbuf[...]), poolw_ref[...], preferred_element_type=jnp.float32)
    ypool_ref[0] = _bf16(y * poolscale_ref[...])

    c, s1, s2 = c_ref[0], s1_ref[0], s2_ref[0]
    half = ROPE_DIM // 2
    for src, dst, scale in ((q_ref, qo_ref, DIFF_QK_DIM ** -0.5 * math.log2(math.e)), (k_ref, ko_ref, None)):
        for lo in range(0, MIX_W, LANES):
            xh = src[0, :, lo:lo + LANES]
            r = xh * c + pltpu.roll(xh, LANES - half, 1) * s1 + pltpu.roll(xh, half, 1) * s2
            if scale is not None:
                r = r * scale
            dst[0, :, lo:lo + LANES] = _bf16(r)
    vt = v_ref[0].T
    ones_row = jnp.where(lax.broadcasted_iota(jnp.int32, (BF16_ROWS, ts), 0) == 0, 1.0, 0.0)
    for h in range(N_DIFF_HEADS):
        vt_ref[0, 0, h * V_ROWS:h * V_ROWS + DIFF_V_DIM, :] = _bf16(vt[h * DIFF_V_DIM:(h + 1) * DIFF_V_DIM, :])
        vt_ref[0, 0, h * V_ROWS + DIFF_V_DIM:(h + 1) * V_ROWS, :] = _bf16(ones_row)


def _mixers(z3, tabs, poolw_bd, pool_scale, pool_tap, pool_win, sconv_w, conf_dw, conf_g, conf_b):
    b, s, _ = z3.shape
    ts = ATT_BLOCK if s % ATT_BLOCK == 0 else s
    nblk = s // ts
    per_halo = ts // HALO

    def col(cidx):
        return pl.BlockSpec((1, ts, MIX_W), lambda bi, i: (bi, i, cidx))

    def halo(cidx):
        return pl.BlockSpec((1, HALO, MIX_W), lambda bi, i: (bi, jnp.maximum(i * per_halo - 1, 0), cidx))

    def const(shape):
        return pl.BlockSpec(shape, lambda bi, i: (0,) * len(shape))

    tab_spec = pl.BlockSpec((1, ts, LANES), lambda bi, i: (bi, i, 0))
    y_spec = pl.BlockSpec((1, ts, MIX_W), lambda bi, i: (bi, i, 0))
    y_shape = jax.ShapeDtypeStruct((b, s, MIX_W), jnp.bfloat16)
    return pl.pallas_call(
        _mixer_kernel,
        grid=(b, nblk),
        in_specs=[col(c) for c in range(9)]
        + [halo(c) for c in (COL_POOL, COL_SCG, COL_SH, COL_CA, COL_CG)]
        + [tab_spec, tab_spec, tab_spec]
        + [const((MIX_W, MIX_W)), const((1, MIX_W)), const((max(POOL_WINDOWS), MIX_W)), const((1, MIX_W)),
           const((SCONV_K, MIX_W)), const((CONF_K, MIX_W)), const((1, MIX_W)), const((1, MIX_W))],
        out_specs=[y_spec, y_spec, y_spec, y_spec, y_spec,
                   pl.BlockSpec((1, 1, N_DIFF_HEADS * V_ROWS, ts), lambda bi, i: (bi, i, 0, 0))],
        out_shape=[y_shape, y_shape, y_shape, y_shape, y_shape,
                   jax.ShapeDtypeStruct((b, nblk, N_DIFF_HEADS * V_ROWS, ts), jnp.bfloat16)],
        scratch_shapes=[pltpu.VMEM((SUBLANES, HALO + ts, MIX_W), jnp.float32)] * 3
        + [pltpu.VMEM((ts, MIX_W), jnp.float32)],
        compiler_params=_params("arbitrary", "arbitrary"),
        name="mixers",
    )(*([z3] * 14), *tabs, poolw_bd, pool_scale, pool_tap, pool_win, sconv_w, conf_dw, conf_g, conf_b)


def _attn_kernel(lam_ref, g_ref, q_ref, k_ref, vt_ref, o_ref, qz_sc, sa_sc, sb_sc, ca_sc, cb_sc, p_sc,
                 m_sc, acc_sc, *, out_scale, lam_init):
    qi = pl.program_id(2)
    tq = q_ref.shape[1]
    q = q_ref[0]
    lane = lax.broadcasted_iota(jnp.int32, q.shape, 1)
    for hm in range(4):
        lo = hm * DIFF_QK_DIM
        qz_sc[hm] = jnp.where((lane >= lo) & (lane < lo + DIFF_QK_DIM), q, jnp.zeros_like(q))

    m_sc[...] = jnp.full(m_sc.shape, -jnp.inf, jnp.float32)
    acc_sc[...] = jnp.zeros(acc_sc.shape, jnp.float32)

    def produce(kb, s_buf, c_buf, diagonal):
        kt = k_ref[0, pl.ds(pl.multiple_of(kb * tq, tq), tq), :]
        for hm in range(4):
            s = lax.dot_general(kt, qz_sc[hm], (((1,), (1,)), ((), ())),
                                preferred_element_type=jnp.float32)
            if diagonal:
                kr = lax.broadcasted_iota(jnp.int32, s.shape, 0)
                qc = lax.broadcasted_iota(jnp.int32, s.shape, 1)
                s = jnp.where(kr <= qc, s, -jnp.inf)
            s_buf[hm] = s
            c_buf[hm] = jnp.max(s, axis=0, keepdims=True)

    def consume(kb, s_buf, c_buf):
        vt = vt_ref[0, kb]
        for hm in range(4):
            m_old = m_sc[hm]
            m_new = jnp.maximum(m_old, c_buf[hm])
            a = jnp.exp2(m_old - m_new)
            for c in range(0, tq, ATT_CHUNK):
                p_sc[hm, c:c + ATT_CHUNK, :] = _bf16(jnp.exp2(s_buf[hm, c:c + ATT_CHUNK, :] - m_new))
            h = hm // 2
            pv = jnp.dot(vt[h * V_ROWS:(h + 1) * V_ROWS, :], p_sc[hm],
                         preferred_element_type=jnp.float32)
            acc_sc[hm] = acc_sc[hm] * a + pv
            m_sc[hm] = m_new

    produce(qi, sa_sc, ca_sc, True)
    pairs = lax.shift_right_logical(qi, 1)

    def body(t, carry):
        produce(2 * t, sb_sc, cb_sc, False)
        consume(jnp.where(t == 0, qi, 2 * t - 1), sa_sc, ca_sc)
        produce(2 * t + 1, sa_sc, ca_sc, False)
        consume(2 * t, sb_sc, cb_sc)
        return carry

    lax.fori_loop(0, pairs, body, 0)
    odd = (qi & 1) == 1

    @pl.when(odd)
    def _():
        produce(qi - 1, sb_sc, cb_sc, False)

    consume(jnp.where(pairs == 0, qi, 2 * pairs - 1), sa_sc, ca_sc)

    @pl.when(odd)
    def _():
        consume(qi - 1, sb_sc, cb_sc)

    lp = lam_ref[...]
    lam = (jnp.exp(jnp.sum(lp[0:1] * lp[1:2], axis=1, keepdims=True))
           - jnp.exp(jnp.sum(lp[2:3] * lp[3:4], axis=1, keepdims=True)) + lam_init)
    outs = []
    def normalised(hm):
        return acc_sc[hm, 0:DIFF_V_DIM, :] / acc_sc[hm, DIFF_V_DIM:DIFF_V_DIM + 1, :]

    for h in range(2):
        o = normalised(2 * h) - lam * normalised(2 * h + 1)
        ms = jnp.mean(o * o, axis=0, keepdims=True)
        outs.append(o * lax.rsqrt(ms + LN_EPS) * g_ref[...] * out_scale)
    o_ref[0] = _bf16(jnp.concatenate(outs, axis=0).T)


def _attention(q_rot, k_rot, vt, lam_params, subln_g, lam_init):
    b, s, _ = q_rot.shape
    nblk, tq = vt.shape[1], vt.shape[3]
    pairs = MIX_W // LANES
    kern = functools.partial(_attn_kernel, out_scale=1.0 - lam_init, lam_init=lam_init)
    return pl.pallas_call(
        kern,
        grid=(b, pairs, nblk),
        in_specs=[pl.BlockSpec((4, DIFF_QK_DIM), lambda bi, hp, qi: (0, 0)),
                  pl.BlockSpec((DIFF_V_DIM, 1), lambda bi, hp, qi: (0, 0)),
                  pl.BlockSpec((1, tq, LANES), lambda bi, hp, qi: (bi, qi, hp)),
                  pl.BlockSpec((1, s, LANES), lambda bi, hp, qi: (bi, 0, hp)),
                  pl.BlockSpec((1, nblk, 2 * V_ROWS, tq), lambda bi, hp, qi: (bi, 0, hp, 0))],
        out_specs=pl.BlockSpec((1, tq, LANES), lambda bi, hp, qi: (bi, qi, hp)),
        out_shape=jax.ShapeDtypeStruct((b, s, MIX_W), jnp.bfloat16),
        scratch_shapes=[pltpu.VMEM((4, tq, LANES), jnp.bfloat16),
                        pltpu.VMEM((4, tq, tq), jnp.float32), pltpu.VMEM((4, tq, tq), jnp.float32),
                        pltpu.VMEM((4, 1, tq), jnp.float32), pltpu.VMEM((4, 1, tq), jnp.float32),
                        pltpu.VMEM((4, tq, tq), jnp.bfloat16),
                        pltpu.VMEM((4, 1, tq), jnp.float32),
                        pltpu.VMEM((4, V_ROWS, tq), jnp.float32)],
        compiler_params=_params("arbitrary", "arbitrary", "arbitrary"),
        name="diff_attention",
    )(lam_params, subln_g[:, None], q_rot, k_rot, vt)


def _out_proj_kernel(yp_ref, ys_ref, ya_ref, yc_ref, x_ref, w_ref, g_ref, b_ref, o_ref):
    acc = None
    for gi, y_ref in enumerate((yp_ref, ys_ref, ya_ref, yc_ref)):
        part = jnp.dot(y_ref[...], w_ref[gi * MIX_W:(gi + 1) * MIX_W, :], preferred_element_type=jnp.float32)
        acc = part if acc is None else acc + part
    o_ref[...] = _layer_norm(DN_ALPHA * x_ref[...] + acc, g_ref[...], b_ref[...])


def _out_proj(ys, x2d, w_bf16, g, b):
    t = x2d.shape[0]
    tm = _tile(t, 512)
    y_spec = pl.BlockSpec((tm, MIX_W), lambda i: (i, 0))
    row = pl.BlockSpec((tm, D_MODEL), lambda i: (i, 0))
    vec = pl.BlockSpec((1, D_MODEL), lambda i: (0, 0))
    return pl.pallas_call(
        _out_proj_kernel,
        grid=(t // tm,),
        in_specs=[y_spec] * 4 + [row, pl.BlockSpec((D_MODEL, D_MODEL), lambda i: (0, 0)), vec, vec],
        out_specs=row,
        out_shape=jax.ShapeDtypeStruct((t, D_MODEL), jnp.float32),
        compiler_params=_params("arbitrary"),
        name="out_proj_ln",
    )(*ys, x2d, w_bf16, g, b)


def _ffn_kernel(x_ref, wg_ref, wu_ref, wd_ref, g_ref, b_ref, o_ref, acc_ref):
    f = pl.program_id(1)
    xb = _bf16(x_ref[...])
    gate = jnp.dot(xb, wg_ref[...], preferred_element_type=jnp.float32)
    up = jnp.dot(xb, wu_ref[...], preferred_element_type=jnp.float32)
    part = jnp.dot(_bf16(gate * _sigmoid(gate) * up), wd_ref[...], preferred_element_type=jnp.float32)

    @pl.when(f == 0)
    def _():
        acc_ref[...] = part

    @pl.when(f > 0)
    def _():
        acc_ref[...] += part

    @pl.when(f == pl.num_programs(1) - 1)
    def _():
        o_ref[...] = _layer_norm(DN_ALPHA * x_ref[...] + acc_ref[...], g_ref[...], b_ref[...])


def _ffn(x2d, wg, wu, wd, g, b):
    t = x2d.shape[0]
    d_ff = wg.shape[1]
    tm = _tile(t, 512)
    tf = d_ff // 2 if (d_ff // 2) % LANES == 0 else d_ff
    row = pl.BlockSpec((tm, D_MODEL), lambda i, f: (i, 0))
    vec = pl.BlockSpec((1, D_MODEL), lambda i, f: (0, 0))
    return pl.pallas_call(
        _ffn_kernel,
        grid=(t // tm, d_ff // tf),
        in_specs=[row,
                  pl.BlockSpec((D_MODEL, tf), lambda i, f: (0, f)),
                  pl.BlockSpec((D_MODEL, tf), lambda i, f: (0, f)),
                  pl.BlockSpec((tf, D_MODEL), lambda i, f: (f, 0)),
                  vec, vec],
        out_specs=row,
        out_shape=jax.ShapeDtypeStruct((t, D_MODEL), jnp.float32),
        scratch_shapes=[pltpu.VMEM((tm, D_MODEL), jnp.float32)],
        compiler_params=_params("arbitrary", "arbitrary"),
        name="ffn_ln",
    )(x2d, wg, wu, wd, g, b)


def _router_kernel(x_ref, wr_ref, idx_ref, gate_ref, cnt_ref, carry_ref):
    @pl.when(pl.program_id(0) == 0)
    def _():
        carry_ref[...] = jnp.zeros_like(carry_ref)

    tr = x_ref.shape[0]
    logits = jnp.dot(x_ref[...], wr_ref[...], preferred_element_type=jnp.float32,
                     precision=lax.Precision.HIGHEST)
    lane = lax.broadcasted_iota(jnp.int32, logits.shape, 1).astype(jnp.float32)
    lg = jnp.where(lane < N_EXPERTS, logits, -jnp.inf)
    m1 = jnp.max(lg, axis=1, keepdims=True)
    i1 = jnp.min(jnp.where(lg == m1, lane, float(LANES)), axis=1, keepdims=True)
    lg2 = jnp.where(lane == i1, -jnp.inf, lg)
    m2 = jnp.max(lg2, axis=1, keepdims=True)
    i2 = jnp.min(jnp.where(lg2 == m2, lane, float(LANES)), axis=1, keepdims=True)
    e2 = jnp.exp(m2 - m1)
    den = 1.0 + e2
    w1 = 1.0 / den
    w2 = e2 / den

    sel1 = lane == i1
    sel2 = lane == i2
    onehot = jnp.where(sel1 | sel2, 1.0, 0.0)
    rr = lax.broadcasted_iota(jnp.int32, (tr, tr), 0)
    cc = lax.broadcasted_iota(jnp.int32, (tr, tr), 1)
    lower = _bf16(jnp.where(cc < rr, 1.0, 0.0))
    carry = carry_ref[0:1, :]
    before = jnp.dot(lower, _bf16(onehot), preferred_element_type=jnp.float32) + carry
    r1 = jnp.sum(jnp.where(sel1, before, 0.0), axis=1, keepdims=True)
    r2 = jnp.sum(jnp.where(sel2, before, 0.0), axis=1, keepdims=True)
    carry_ref[...] = jnp.broadcast_to(carry + jnp.sum(onehot, axis=0, keepdims=True), carry_ref.shape)
    cnt_ref[...] = carry_ref[...]
    info = jnp.where(lane == 0.0, i1, jnp.where(lane == 1.0, i2,
                     jnp.where(lane == 2.0, r1, jnp.where(lane == 3.0, r2, 0.0))))
    idx_ref[...] = info.astype(jnp.int32)
    gate_ref[...] = jnp.where(lane == 0.0, w1, jnp.where(lane == 1.0, w2, 0.0))


def _router(x2d, wr_pad):
    t = x2d.shape[0]
    tr = _tile(t, 512)
    wide = pl.BlockSpec((tr, LANES), lambda i: (i, 0))
    return pl.pallas_call(
        _router_kernel,
        grid=(t // tr,),
        in_specs=[pl.BlockSpec((tr, D_MODEL), lambda i: (i, 0)),
                  pl.BlockSpec((D_MODEL, LANES), lambda i: (0, 0))],
        out_specs=[wide, wide, pl.BlockSpec((SUBLANES, LANES), lambda i: (0, 0))],
        out_shape=[jax.ShapeDtypeStruct((t, LANES), jnp.int32),
                   jax.ShapeDtypeStruct((t, LANES), jnp.float32),
                   jax.ShapeDtypeStruct((SUBLANES, LANES), jnp.float32)],
        scratch_shapes=[pltpu.VMEM((SUBLANES, LANES), jnp.float32)],
        compiler_params=_params("arbitrary"),
        name="router_rank",
    )(x2d, wr_pad)


def _dispatch_kernel(ends_ref, pos_ref, x_ref, xs_ref, zero_ref, sem, zsem, *, tm):
    td = x_ref.shape[0]

    @pl.when(pl.program_id(0) == 0)
    def _():
        zero_ref[...] = jnp.zeros_like(zero_ref)

        def tile_copy(start):
            return pltpu.make_async_copy(zero_ref, xs_ref.at[pl.ds(pl.multiple_of(start, tm), tm), :], zsem)

        def tail(fn):
            def body(j, carry):
                fn(tile_copy(j * tm))
                return carry
            lax.fori_loop(ends_ref[N_EXPERTS + 1], xs_ref.shape[0] // tm, body, 0)

        for e in range(N_EXPERTS):
            @pl.when(ends_ref[e + 1] > ends_ref[e])
            def _():
                tile_copy(ends_ref[e + 1] - tm).start()
        tail(lambda cp: cp.start())
        for e in range(N_EXPERTS):
            @pl.when(ends_ref[e + 1] > ends_ref[e])
            def _():
                tile_copy(ends_ref[e + 1] - tm).wait()
        tail(lambda cp: cp.wait())

    def row_copy(r, k):
        p = pos_ref[0, 0, 2 * r + k]
        return pltpu.make_async_copy(x_ref.at[pl.ds(r, 1), :], xs_ref.at[pl.ds(p, 1), :], sem)

    def start(r, carry):
        row_copy(r, 0).start()
        row_copy(r, 1).start()
        return carry

    def wait(r, carry):
        row_copy(r, 0).wait()
        row_copy(r, 1).wait()
        return carry

    lax.fori_loop(0, td, start, 0)
    lax.fori_loop(0, td, wait, 0)


def _dispatch(x2d, pos, ends0, n_rows, tm):
    t = x2d.shape[0]
    td = _tile(t, 256)
    grid_spec = pltpu.PrefetchScalarGridSpec(
        num_scalar_prefetch=1,
        grid=(t // td,),
        in_specs=[pl.BlockSpec((1, 1, 2 * td), lambda i, ends: (i, 0, 0), memory_space=pltpu.SMEM),
                  pl.BlockSpec((td, D_MODEL), lambda i, ends: (i, 0))],
        out_specs=pl.BlockSpec(memory_space=pl.ANY),
        scratch_shapes=[pltpu.VMEM((tm, D_MODEL), jnp.float32),
                        pltpu.SemaphoreType.DMA(()), pltpu.SemaphoreType.DMA(())],
    )
    return pl.pallas_call(
        functools.partial(_dispatch_kernel, tm=tm),
        grid_spec=grid_spec,
        out_shape=jax.ShapeDtypeStruct((n_rows, D_MODEL), jnp.float32),
        compiler_params=_params("arbitrary"),
        name="moe_dispatch",
    )(ends0, pos.reshape(t // td, 1, 2 * td), x2d)


def _moe_ffn_kernel(te_ref, nu_ref, xs_ref, wg_ref, wu_ref, wd_ref, ys_ref):
    i = pl.program_id(0)
    f = pl.program_id(1)

    @pl.when(i < nu_ref[0])
    def _():
        xb = _bf16(xs_ref[...])
        gate = jnp.dot(xb, wg_ref[0], preferred_element_type=jnp.float32)
        up = jnp.dot(xb, wu_ref[0], preferred_element_type=jnp.float32)
        part = jnp.dot(_bf16(gate * _sigmoid(gate) * up), wd_ref[0], preferred_element_type=jnp.float32)

        @pl.when(f == 0)
        def _():
            ys_ref[...] = part

        @pl.when(f > 0)
        def _():
            ys_ref[...] += part

    @pl.when(i >= nu_ref[0])
    def _():
        ys_ref[...] = jnp.zeros_like(ys_ref)


def _moe_ffn(xs, wg, wu, wd, tile_expert, n_used, tm):
    n_rows = xs.shape[0]
    n_tiles = n_rows // tm
    d_ff = wg.shape[2]
    tf = _tile(d_ff, 512)
    nf = d_ff // tf

    def row_idx(i, f, te, nu):
        return (jnp.minimum(i, nu[0] - 1), 0)

    def out_idx(i, f, te, nu):
        return (i, 0)

    def f_idx(i, f, nu):
        return jnp.where(i < nu[0], f, nf - 1)

    def up_idx(i, f, te, nu):
        return (te[jnp.minimum(i, nu[0] - 1)], 0, f_idx(i, f, nu))

    def down_idx(i, f, te, nu):
        return (te[jnp.minimum(i, nu[0] - 1)], f_idx(i, f, nu), 0)

    grid_spec = pltpu.PrefetchScalarGridSpec(
        num_scalar_prefetch=2,
        grid=(n_tiles, nf),
        in_specs=[pl.BlockSpec((tm, D_MODEL), row_idx),
                  pl.BlockSpec((1, D_MODEL, tf), up_idx),
                  pl.BlockSpec((1, D_MODEL, tf), up_idx),
                  pl.BlockSpec((1, tf, D_MODEL), down_idx)],
        out_specs=pl.BlockSpec((tm, D_MODEL), out_idx),
    )
    return pl.pallas_call(
        _moe_ffn_kernel,
        grid_spec=grid_spec,
        out_shape=jax.ShapeDtypeStruct((n_rows, D_MODEL), jnp.float32),
        compiler_params=_params("arbitrary", "arbitrary"),
        name="moe_grouped_ffn",
    )(tile_expert, n_used, xs, wg, wu, wd)


def _combine_kernel(pos_ref, gate_ref, x_ref, ys_ref, g_ref, b_ref, o_ref, buf, sem):
    tc = x_ref.shape[0]

    def row_copy(r, k):
        p = pos_ref[0, 0, 2 * r + k]
        return pltpu.make_async_copy(ys_ref.at[pl.ds(p, 1), :], buf.at[k, pl.ds(r, 1), :], sem)

    def start(r, carry):
        row_copy(r, 0).start()
        row_copy(r, 1).start()
        return carry

    def wait(r, carry):
        row_copy(r, 0).wait()
        row_copy(r, 1).wait()
        return carry

    lax.fori_loop(0, tc, start, 0)
    lax.fori_loop(0, tc, wait, 0)
    gates = gate_ref[...]
    f = gates[:, 0:1] * buf[0] + gates[:, 1:2] * buf[1]
    o_ref[...] = _layer_norm(DN_ALPHA * x_ref[...] + f, g_ref[...], b_ref[...])


def _combine(x2d, pos, gates, ys, g, b):
    t = x2d.shape[0]
    tc = _tile(t, 256)
    row = pl.BlockSpec((tc, D_MODEL), lambda i: (i, 0))
    vec = pl.BlockSpec((1, D_MODEL), lambda i: (0, 0))
    return pl.pallas_call(
        _combine_kernel,
        grid=(t // tc,),
        in_specs=[pl.BlockSpec((1, 1, 2 * tc), lambda i: (i, 0, 0), memory_space=pltpu.SMEM),
                  pl.BlockSpec((tc, LANES), lambda i: (i, 0)),
                  row,
                  pl.BlockSpec(memory_space=pl.ANY),
                  vec, vec],
        out_specs=row,
        out_shape=jax.ShapeDtypeStruct((t, D_MODEL), jnp.float32),
        scratch_shapes=[pltpu.VMEM((2, tc, D_MODEL), jnp.float32), pltpu.SemaphoreType.DMA(())],
        compiler_params=_params("arbitrary"),
        name="moe_combine_ln",
    )(pos.reshape(t // tc, 1, 2 * tc), gates, x2d, ys, g, b)


def _moe(x2d, wr, wg, wu, wd, g, b):
    t = x2d.shape[0]
    tm = 1024 if t >= 8192 else 128
    wr_pad = jnp.zeros((D_MODEL, LANES), jnp.float32).at[:, :N_EXPERTS].set(wr)
    idx, gates, cnt = _router(x2d, wr_pad)
    counts = cnt[0, :N_EXPERTS].astype(jnp.int32)
    padded = (counts + tm - 1) // tm * tm
    ends = jnp.cumsum(padded)
    starts = ends - padded
    pos = (starts[idx[:, 0:2]] + idx[:, 2:4]).astype(jnp.int32)
    n_tiles = (2 * t) // tm + N_EXPERTS
    tile_start = jnp.arange(n_tiles, dtype=jnp.int32) * tm
    tile_expert = jnp.minimum(jnp.sum(tile_start[:, None] >= ends[None, :], axis=1), N_EXPERTS - 1).astype(jnp.int32)
    n_used = (ends[-1:] // tm).astype(jnp.int32)
    ends0 = jnp.concatenate([jnp.zeros((1,), jnp.int32), ends.astype(jnp.int32), n_used])
    xs = _dispatch(x2d, pos, ends0, n_tiles * tm, tm)
    ys = _moe_ffn(xs, wg, wu, wd, tile_expert, n_used, tm)
    return _combine(x2d, pos, gates, ys, g, b)


def _pool_constants(pool_w):
    n_g = len(POOL_WINDOWS)
    bd = jnp.zeros((MIX_W, MIX_W), jnp.float32)
    for gi in range(n_g):
        sl = slice(gi * POOL_GROUP, (gi + 1) * POOL_GROUP)
        bd = bd.at[sl, sl].set(pool_w[gi])
    win = jnp.repeat(jnp.asarray(POOL_WINDOWS, jnp.float32), POOL_GROUP)[None, :]
    lag = (max(POOL_WINDOWS) - 1 - jnp.arange(max(POOL_WINDOWS), dtype=jnp.float32))[:, None]
    tap = jnp.where(lag < win, 1.0, 0.0).astype(jnp.float32)
    return _bf16(bd), tap, win


def kernel(x, positions, w_in, w_o, pool_w, pool_scale, sconv_w, diff_lambda, diff_subln_g,
           conf_dw, conf_ln_g, conf_ln_b, ln1_g, ln1_b, ln2_g, ln2_b,
           ffn_wg, ffn_wu, ffn_wd, router_w, moe_wg, moe_wu, moe_wd):
    b, s, d = x.shape
    assert d == D_MODEL and s % HALO == 0
    t = b * s
    tabs = _rope_tables(positions)
    x2d = x.reshape(t, d)
    for l in range(DEPTH):
        lam_init = 0.8 - 0.6 * math.exp(-0.3 * l)
        z = _in_proj(x2d, _bf16(w_in[l]))
        poolw_bd, pool_tap, pool_win = _pool_constants(pool_w[l])
        y_pool, y_sconv, y_conf, q_rot, k_rot, vt = _mixers(
            z.reshape(b, s, IN_COLS), tabs, poolw_bd, pool_scale[l][None, :], pool_tap, pool_win,
            sconv_w[l], conf_dw[l], conf_ln_g[l][None, :], conf_ln_b[l][None, :])
        y_attn = _attention(q_rot, k_rot, vt, diff_lambda[l], diff_subln_g[l], lam_init)
        ys = [y.reshape(t, MIX_W) for y in (y_pool, y_sconv, y_attn, y_conf)]
        x2d = _out_proj(ys, x2d, _bf16(w_o[l]), ln1_g[l][None, :], ln1_b[l][None, :])
        j = l // 2
        if l % 2 == 0:
            x2d = _ffn(x2d, _bf16(ffn_wg[j]), _bf16(ffn_wu[j]), _bf16(ffn_wd[j]),
                       ln2_g[l][None, :], ln2_b[l][None, :])
        else:
            x2d = _moe(x2d, router_w[j], _bf16(moe_wg[j]), _bf16(moe_wu[j]), _bf16(moe_wd[j]),
                       ln2_g[l][None, :], ln2_b[l][None, :])
    return x2d.reshape(b, s, d)
```

```python
import functools
import math

import jax
import jax.numpy as jnp
from jax import lax
from jax.experimental import pallas as pl
from jax.experimental.pallas import tpu as pltpu

D_MODEL = 1024
DEPTH = 4
MIX_W = D_MODEL // 4
POOL_WINDOWS = (2, 4, 8, 16)
POOL_GROUP = MIX_W // len(POOL_WINDOWS)
SCONV_K = 3
N_DIFF_HEADS = 4
DIFF_V_DIM = MIX_W // N_DIFF_HEADS
DIFF_QK_DIM = DIFF_V_DIM // 2
ROPE_DIM = DIFF_QK_DIM // 4
ROPE_THETA = 500000.0
CONF_K = 31
N_EXPERTS = 8
DN_ALPHA = (2 * DEPTH) ** 0.25
LN_EPS = 1e-5
IN_COLS = 9 * MIX_W
COL_POOL, COL_SB, COL_SCG, COL_SH, COL_Q, COL_K, COL_V, COL_CA, COL_CG = range(9)

LANES = 128
SUBLANES = 8
VMEM_LIMIT_BYTES = 48 * 1024 * 1024

HALO = 32
CONV_ROWS = 64
ATT_Q_BLOCK = 256
ATT_KV_BLOCK = 512
ATT_PV_K = 256
ATT_CHUNK = 64
DMA_UNROLL = 4
BF16_ROWS = 2 * SUBLANES
V_ROWS = DIFF_V_DIM + BF16_ROWS


def _tile(n, pref):
    if n <= pref:
        return n
    t = pref - pref % SUBLANES
    while t > SUBLANES and n % t:
        t -= SUBLANES
    assert n % t == 0, (n, pref)
    return t


def _params(*sem):
    return pltpu.CompilerParams(dimension_semantics=sem, vmem_limit_bytes=VMEM_LIMIT_BYTES)


def _layer_norm(v, g, b):
    mu = jnp.mean(v, axis=-1, keepdims=True)
    vc = v - mu
    var = jnp.mean(vc * vc, axis=-1, keepdims=True)
    return vc * lax.rsqrt(var + LN_EPS) * g + b


def _sigmoid(v):
    return 1.0 / (1.0 + jnp.exp(-v))


def _bf16(v):
    return v.astype(jnp.bfloat16)


def _in_proj_kernel(x_ref, w_ref, z_ref):
    xb = _bf16(x_ref[...])
    step = 3 * MIX_W
    for c in range(0, IN_COLS, step):
        z_ref[:, c:c + step] = jnp.dot(xb, w_ref[:, c:c + step], preferred_element_type=jnp.float32)


def _in_proj(x2d, w_bf16):
    t = x2d.shape[0]
    tm = _tile(t, 512)
    return pl.pallas_call(
        _in_proj_kernel,
        grid=(t // tm,),
        in_specs=[pl.BlockSpec((tm, D_MODEL), lambda i: (i, 0)),
                  pl.BlockSpec((D_MODEL, IN_COLS), lambda i: (0, 0))],
        out_specs=pl.BlockSpec((tm, IN_COLS), lambda i: (i, 0)),
        out_shape=jax.ShapeDtypeStruct((t, IN_COLS), jnp.float32),
        compiler_params=_params("arbitrary"),
        name="in_proj",
    )(x2d, w_bf16)


def _rope_table_kernel(pos_ref, freq_ref, c_ref, s1_ref, s2_ref):
    ang = pos_ref[0].astype(jnp.float32) * freq_ref[...]
    d = lax.broadcasted_iota(jnp.int32, ang.shape, 1) & (DIFF_QK_DIM - 1)
    cs = jnp.cos(ang)
    sn = jnp.sin(ang)
    half = ROPE_DIM // 2
    c_ref[0] = jnp.where(d < ROPE_DIM, cs, 1.0)
    s1_ref[0] = jnp.where(d < half, -sn, 0.0)
    s2_ref[0] = jnp.where((d >= half) & (d < ROPE_DIM), sn, 0.0)


def _rope_tables(positions):
    b, s = positions.shape
    ts = _tile(s, 512)
    inv = ROPE_THETA ** (-jnp.arange(0, ROPE_DIM, 2, dtype=jnp.float32) / ROPE_DIM)
    d = jnp.arange(LANES) % DIFF_QK_DIM
    freq = jnp.where(d < ROPE_DIM, inv[d % (ROPE_DIM // 2)], 0.0).astype(jnp.float32)[None, :]
    tab = jax.ShapeDtypeStruct((b, s, LANES), jnp.float32)
    spec = pl.BlockSpec((1, ts, LANES), lambda bi, i: (bi, i, 0))
    return pl.pallas_call(
        _rope_table_kernel,
        grid=(b, s // ts),
        in_specs=[pl.BlockSpec((1, ts, 1), lambda bi, i: (bi, i, 0)),
                  pl.BlockSpec((1, LANES), lambda bi, i: (0, 0))],
        out_specs=[spec, spec, spec],
        out_shape=[tab, tab, tab],
        compiler_params=_params("arbitrary", "arbitrary"),
        name="rope_tables",
    )(positions[:, :, None], freq)


def _fill_shifted(buf_ref, residues):
    n = buf_ref.shape[1] - SUBLANES
    for r in residues:
        buf_ref[r, 0:n, :] = buf_ref[0, r:r + n, :]


def _tap_residues(n_taps):
    return sorted({(j - (n_taps - 1)) % SUBLANES for j in range(n_taps)} - {0})


def _causal_taps(buf_ref, w_ref, n_taps, r0):
    acc = None
    for j in range(n_taps):
        start = HALO + r0 - (n_taps - 1) + j
        r = start % SUBLANES
        term = w_ref[j:j + 1, :] * buf_ref[r, start - r:start - r + CONV_ROWS, :]
        acc = term if acc is None else acc + term
    return acc


def _mixer_kernel(p_ref, sb_ref, scg_ref, sh_ref, q_ref, k_ref, v_ref, ca_ref, cg_ref,
                  p_halo, scg_halo, sh_halo, ca_halo, cg_halo,
                  c_ref, s1_ref, s2_ref,
                  poolw_ref, poolscale_ref, pooltap_ref, poolwin_ref,
                  sconvw_ref, confw_ref, confg_ref, confb_ref,
                  ypool_ref, ysconv_ref, yconf_ref, qo_ref, ko_ref, vt_ref,
                  pbuf, gbuf, ubuf, dbuf):
    i = pl.program_id(1)
    ts = p_ref.shape[1]
    keep = jnp.where(i == 0, 0.0, 1.0)

    pbuf[0, 0:HALO, :] = p_halo[0] * keep
    pbuf[0, HALO:, :] = p_ref[0]
    gbuf[0, 0:HALO, :] = scg_halo[0] * sh_halo[0] * keep
    gbuf[0, HALO:, :] = scg_ref[0] * sh_ref[0]
    ubuf[0, 0:HALO, :] = ca_halo[0] * _sigmoid(cg_halo[0]) * keep
    ubuf[0, HALO:, :] = ca_ref[0] * _sigmoid(cg_ref[0])
    n_pool_taps = max(POOL_WINDOWS)
    _fill_shifted(pbuf, _tap_residues(n_pool_taps))
    _fill_shifted(gbuf, _tap_residues(SCONV_K))
    _fill_shifted(ubuf, _tap_residues(CONF_K))

    for r0 in range(0, ts, CONV_ROWS):
        rows = slice(r0, r0 + CONV_ROWS)
        wsum = _causal_taps(pbuf, pooltap_ref, n_pool_taps, r0)
        tpos = (i * ts + r0 + 1 + lax.broadcasted_iota(jnp.int32, (CONV_ROWS, MIX_W), 0)).astype(jnp.float32)
        cnt = jnp.minimum(tpos, poolwin_ref[...])
        dbuf[rows, :] = wsum / cnt - pbuf[0, HALO + r0:HALO + r0 + CONV_ROWS, :]
        sc = _causal_taps(gbuf, sconvw_ref, SCONV_K, r0)
        ysconv_ref[0, rows, :] = _bf16(sb_ref[0, rows, :] * sc)
        u = _causal_taps(ubuf, confw_ref, CONF_K, r0)
        u = _layer_norm(u, confg_ref[...], confb_ref[...])
        yconf_ref[0, rows, :] = _bf16(u * _sigmoid(u))

    y = jnp.dot(_bf16(dbuf[...]), poolw_ref[...], preferred_element_type=jnp.float32)
    ypool_ref[0] = _bf16(y * poolscale_ref[...])

    c, s1, s2 = c_ref[0], s1_ref[0], s2_ref[0]
    half = ROPE_DIM // 2
    for src, dst, scale in ((q_ref, qo_ref, DIFF_QK_DIM ** -0.5 * math.log2(math.e)), (k_ref, ko_ref, None)):
        for lo in range(0, MIX_W, LANES):
            xh = src[0, :, lo:lo + LANES]
            r = xh * c + pltpu.roll(xh, LANES - half, 1) * s1 + pltpu.roll(xh, half, 1) * s2
            if scale is not None:
                r = r * scale
            dst[0, :, lo:lo + LANES] = _bf16(r)
    vt = v_ref[0].T
    ones_row = jnp.where(lax.broadcasted_iota(jnp.int32, (BF16_ROWS, ts), 0) == 0, 1.0, 0.0)
    for h in range(N_DIFF_HEADS):
        vt_ref[0, 0, h * V_ROWS:h * V_ROWS + DIFF_V_DIM, :] = _bf16(vt[h * DIFF_V_DIM:(h + 1) * DIFF_V_DIM, :])
        vt_ref[0, 0, h * V_ROWS + DIFF_V_DIM:(h + 1) * V_ROWS, :] = _bf16(ones_row)


def _mixers(z3, tabs, poolw_bd, pool_scale, pool_tap, pool_win, sconv_w, conf_dw, conf_g, conf_b):
    b, s, _ = z3.shape
    ts = ATT_KV_BLOCK if s % ATT_KV_BLOCK == 0 else s
    nblk = s // ts
    per_halo = ts // HALO

    def col(cidx):
        return pl.BlockSpec((1, ts, MIX_W), lambda bi, i: (bi, i, cidx))

    def halo(cidx):
        return pl.BlockSpec((1, HALO, MIX_W), lambda bi, i: (bi, jnp.maximum(i * per_halo - 1, 0), cidx))

    def const(shape):
        return pl.BlockSpec(shape, lambda bi, i: (0,) * len(shape))

    tab_spec = pl.BlockSpec((1, ts, LANES), lambda bi, i: (bi, i, 0))
    y_spec = pl.BlockSpec((1, ts, MIX_W), lambda bi, i: (bi, i, 0))
    y_shape = jax.ShapeDtypeStruct((b, s, MIX_W), jnp.bfloat16)
    return pl.pallas_call(
        _mixer_kernel,
        grid=(b, nblk),
        in_specs=[col(c) for c in range(9)]
        + [halo(c) for c in (COL_POOL, COL_SCG, COL_SH, COL_CA, COL_CG)]
        + [tab_spec, tab_spec, tab_spec]
        + [const((MIX_W, MIX_W)), const((1, MIX_W)), const((max(POOL_WINDOWS), MIX_W)), const((1, MIX_W)),
           const((SCONV_K, MIX_W)), const((CONF_K, MIX_W)), const((1, MIX_W)), const((1, MIX_W))],
        out_specs=[y_spec, y_spec, y_spec, y_spec, y_spec,
                   pl.BlockSpec((1, 1, N_DIFF_HEADS * V_ROWS, ts), lambda bi, i: (bi, i, 0, 0))],
        out_shape=[y_shape, y_shape, y_shape, y_shape, y_shape,
                   jax.ShapeDtypeStruct((b, nblk, N_DIFF_HEADS * V_ROWS, ts), jnp.bfloat16)],
        scratch_shapes=[pltpu.VMEM((SUBLANES, HALO + ts, MIX_W), jnp.float32)] * 3
        + [pltpu.VMEM((ts, MIX_W), jnp.float32)],
        compiler_params=_params("arbitrary", "arbitrary"),
        name="mixers",
    )(*([z3] * 14), *tabs, poolw_bd, pool_scale, pool_tap, pool_win, sconv_w, conf_dw, conf_g, conf_b)


def _attn_kernel(lam_ref, g_ref, q_ref, k_ref, vt_ref, o_ref, qz_sc, sa_sc, sb_sc, ca_sc, cb_sc,
                 m_sc, acc_sc, *, out_scale, lam_init):
    qi = pl.program_id(2)
    tq = q_ref.shape[1]
    q = q_ref[0]
    lane = lax.broadcasted_iota(jnp.int32, q.shape, 1)
    for hm in range(4):
        lo = hm * DIFF_QK_DIM
        qz_sc[hm] = jnp.where((lane >= lo) & (lane < lo + DIFF_QK_DIM), q, jnp.zeros_like(q))

    m_sc[...] = jnp.full(m_sc.shape, -jnp.inf, jnp.float32)
    acc_sc[...] = jnp.zeros(acc_sc.shape, jnp.float32)

    tk = vt_ref.shape[3]
    per_kv = tk // tq
    diag = qi // per_kv
    q_off = (qi - diag * per_kv) * tq

    def produce(kb, s_buf, c_buf, diagonal):
        kt = k_ref[0, pl.ds(pl.multiple_of(kb * tk, tk), tk), :]
        for hm in range(4):
            s = lax.dot_general(kt, qz_sc[hm], (((1,), (1,)), ((), ())),
                                preferred_element_type=jnp.float32)
            if diagonal:
                kr = lax.broadcasted_iota(jnp.int32, s.shape, 0)
                qc = lax.broadcasted_iota(jnp.int32, s.shape, 1)
                s = jnp.where(kr <= qc + q_off, s, -jnp.inf)
            s_buf[hm] = s
            c_buf[hm] = jnp.max(s, axis=0, keepdims=True)

    def consume(kb, s_buf, c_buf):
        vt = vt_ref[0, kb]
        for hm in range(4):
            m_old = m_sc[hm]
            m_new = jnp.maximum(m_old, c_buf[hm])
            a = jnp.exp2(m_old - m_new)
            h = hm // 2
            pv = None
            for k0 in range(0, tk, ATT_PV_K):
                p = jnp.concatenate([_bf16(jnp.exp2(s_buf[hm, c:c + ATT_CHUNK, :] - m_new))
                                     for c in range(k0, k0 + ATT_PV_K, ATT_CHUNK)], axis=0)
                part = jnp.dot(vt[h * V_ROWS:(h + 1) * V_ROWS, k0:k0 + ATT_PV_K], p,
                               preferred_element_type=jnp.float32)
                pv = part if pv is None else pv + part
            acc_sc[hm] = acc_sc[hm] * a + pv
            m_sc[hm] = m_new

    produce(diag, sa_sc, ca_sc, True)
    pairs = lax.shift_right_logical(diag, 1)

    def body(t, carry):
        produce(2 * t, sb_sc, cb_sc, False)
        consume(jnp.where(t == 0, diag, 2 * t - 1), sa_sc, ca_sc)
        produce(2 * t + 1, sa_sc, ca_sc, False)
        consume(2 * t, sb_sc, cb_sc)
        return carry

    lax.fori_loop(0, pairs, body, 0)
    odd = (diag & 1) == 1

    @pl.when(odd)
    def _():
        produce(diag - 1, sb_sc, cb_sc, False)

    consume(jnp.where(pairs == 0, diag, 2 * pairs - 1), sa_sc, ca_sc)

    @pl.when(odd)
    def _():
        consume(diag - 1, sb_sc, cb_sc)

    lp = lam_ref[...]
    lam = (jnp.exp(jnp.sum(lp[0:1] * lp[1:2], axis=1, keepdims=True))
           - jnp.exp(jnp.sum(lp[2:3] * lp[3:4], axis=1, keepdims=True)) + lam_init)
    outs = []
    def normalised(hm):
        return acc_sc[hm, 0:DIFF_V_DIM, :] / acc_sc[hm, DIFF_V_DIM:DIFF_V_DIM + 1, :]

    for h in range(2):
        o = normalised(2 * h) - lam * normalised(2 * h + 1)
        ms = jnp.mean(o * o, axis=0, keepdims=True)
        outs.append(o * lax.rsqrt(ms + LN_EPS) * g_ref[...] * out_scale)
    o_ref[0] = _bf16(jnp.concatenate(outs, axis=0).T)


def _attention(q_rot, k_rot, vt, lam_params, subln_g, lam_init):
    b, s, _ = q_rot.shape
    nkv, tk = vt.shape[1], vt.shape[3]
    tq = min(ATT_Q_BLOCK, tk)
    assert tk % tq == 0 and tk % ATT_PV_K == 0
    pairs = MIX_W // LANES
    kern = functools.partial(_attn_kernel, out_scale=1.0 - lam_init, lam_init=lam_init)
    return pl.pallas_call(
        kern,
        grid=(b, pairs, s // tq),
        in_specs=[pl.BlockSpec((4, DIFF_QK_DIM), lambda bi, hp, qi: (0, 0)),
                  pl.BlockSpec((DIFF_V_DIM, 1), lambda bi, hp, qi: (0, 0)),
                  pl.BlockSpec((1, tq, LANES), lambda bi, hp, qi: (bi, qi, hp)),
                  pl.BlockSpec((1, s, LANES), lambda bi, hp, qi: (bi, 0, hp)),
                  pl.BlockSpec((1, nkv, 2 * V_ROWS, tk), lambda bi, hp, qi: (bi, 0, hp, 0))],
        out_specs=pl.BlockSpec((1, tq, LANES), lambda bi, hp, qi: (bi, qi, hp)),
        out_shape=jax.ShapeDtypeStruct((b, s, MIX_W), jnp.bfloat16),
        scratch_shapes=[pltpu.VMEM((4, tq, LANES), jnp.bfloat16),
                        pltpu.VMEM((4, tk, tq), jnp.float32), pltpu.VMEM((4, tk, tq), jnp.float32),
                        pltpu.VMEM((4, 1, tq), jnp.float32), pltpu.VMEM((4, 1, tq), jnp.float32),
                        pltpu.VMEM((4, 1, tq), jnp.float32),
                        pltpu.VMEM((4, V_ROWS, tq), jnp.float32)],
        compiler_params=_params("arbitrary", "arbitrary", "arbitrary"),
        name="diff_attention",
    )(lam_params, subln_g[:, None], q_rot, k_rot, vt)


def _out_proj_kernel(yp_ref, ys_ref, ya_ref, yc_ref, x_ref, w_ref, g_ref, b_ref, o_ref):
    acc = None
    for gi, y_ref in enumerate((yp_ref, ys_ref, ya_ref, yc_ref)):
        part = jnp.dot(y_ref[...], w_ref[gi * MIX_W:(gi + 1) * MIX_W, :], preferred_element_type=jnp.float32)
        acc = part if acc is None else acc + part
    o_ref[...] = _layer_norm(DN_ALPHA * x_ref[...] + acc, g_ref[...], b_ref[...])


def _out_proj(ys, x2d, w_bf16, g, b):
    t = x2d.shape[0]
    tm = _tile(t, 512)
    y_spec = pl.BlockSpec((tm, MIX_W), lambda i: (i, 0))
    row = pl.BlockSpec((tm, D_MODEL), lambda i: (i, 0))
    vec = pl.BlockSpec((1, D_MODEL), lambda i: (0, 0))
    return pl.pallas_call(
        _out_proj_kernel,
        grid=(t // tm,),
        in_specs=[y_spec] * 4 + [row, pl.BlockSpec((D_MODEL, D_MODEL), lambda i: (0, 0)), vec, vec],
        out_specs=row,
        out_shape=jax.ShapeDtypeStruct((t, D_MODEL), jnp.float32),
        compiler_params=_params("arbitrary"),
        name="out_proj_ln",
    )(*ys, x2d, w_bf16, g, b)


def _ffn_kernel(x_ref, wg_ref, wu_ref, wd_ref, g_ref, b_ref, o_ref, acc_ref):
    f = pl.program_id(1)
    xb = _bf16(x_ref[...])
    gate = jnp.dot(xb, wg_ref[...], preferred_element_type=jnp.float32)
    up = jnp.dot(xb, wu_ref[...], preferred_element_type=jnp.float32)
    part = jnp.dot(_bf16(gate * _sigmoid(gate) * up), wd_ref[...], preferred_element_type=jnp.float32)

    @pl.when(f == 0)
    def _():
        acc_ref[...] = part

    @pl.when(f > 0)
    def _():
        acc_ref[...] += part

    @pl.when(f == pl.num_programs(1) - 1)
    def _():
        o_ref[...] = _layer_norm(DN_ALPHA * x_ref[...] + acc_ref[...], g_ref[...], b_ref[...])


def _ffn(x2d, wg, wu, wd, g, b):
    t = x2d.shape[0]
    d_ff = wg.shape[1]
    tm = _tile(t, 512)
    tf = d_ff // 2 if (d_ff // 2) % LANES == 0 else d_ff
    row = pl.BlockSpec((tm, D_MODEL), lambda i, f: (i, 0))
    vec = pl.BlockSpec((1, D_MODEL), lambda i, f: (0, 0))
    return pl.pallas_call(
        _ffn_kernel,
        grid=(t // tm, d_ff // tf),
        in_specs=[row,
                  pl.BlockSpec((D_MODEL, tf), lambda i, f: (0, f)),
                  pl.BlockSpec((D_MODEL, tf), lambda i, f: (0, f)),
                  pl.BlockSpec((tf, D_MODEL), lambda i, f: (f, 0)),
                  vec, vec],
        out_specs=row,
        out_shape=jax.ShapeDtypeStruct((t, D_MODEL), jnp.float32),
        scratch_shapes=[pltpu.VMEM((tm, D_MODEL), jnp.float32)],
        compiler_params=_params("arbitrary", "arbitrary"),
        name="ffn_ln",
    )(x2d, wg, wu, wd, g, b)


def _router_kernel(x_ref, wr_ref, idx_ref, gate_ref, cnt_ref, carry_ref):
    @pl.when(pl.program_id(0) == 0)
    def _():
        carry_ref[...] = jnp.zeros_like(carry_ref)

    tr = x_ref.shape[0]
    logits = jnp.dot(x_ref[...], wr_ref[...], preferred_element_type=jnp.float32,
                     precision=lax.Precision.HIGHEST)
    lane = lax.broadcasted_iota(jnp.int32, logits.shape, 1).astype(jnp.float32)
    lg = jnp.where(lane < N_EXPERTS, logits, -jnp.inf)
    m1 = jnp.max(lg, axis=1, keepdims=True)
    i1 = jnp.min(jnp.where(lg == m1, lane, float(LANES)), axis=1, keepdims=True)
    lg2 = jnp.where(lane == i1, -jnp.inf, lg)
    m2 = jnp.max(lg2, axis=1, keepdims=True)
    i2 = jnp.min(jnp.where(lg2 == m2, lane, float(LANES)), axis=1, keepdims=True)
    e2 = jnp.exp(m2 - m1)
    den = 1.0 + e2
    w1 = 1.0 / den
    w2 = e2 / den

    sel1 = lane == i1
    sel2 = lane == i2
    onehot = jnp.where(sel1 | sel2, 1.0, 0.0)
    rr = lax.broadcasted_iota(jnp.int32, (tr, tr), 0)
    cc = lax.broadcasted_iota(jnp.int32, (tr, tr), 1)
    lower = _bf16(jnp.where(cc < rr, 1.0, 0.0))
    carry = carry_ref[0:1, :]
    before = jnp.dot(lower, _bf16(onehot), preferred_element_type=jnp.float32) + carry
    r1 = jnp.sum(jnp.where(sel1, before, 0.0), axis=1, keepdims=True)
    r2 = jnp.sum(jnp.where(sel2, before, 0.0), axis=1, keepdims=True)
    carry_ref[...] = jnp.broadcast_to(carry + jnp.sum(onehot, axis=0, keepdims=True), carry_ref.shape)
    cnt_ref[...] = carry_ref[...]
    info = jnp.where(lane == 0.0, i1, jnp.where(lane == 1.0, i2,
                     jnp.where(lane == 2.0, r1, jnp.where(lane == 3.0, r2, 0.0))))
    idx_ref[...] = info.astype(jnp.int32)
    gate_ref[...] = jnp.where(lane == 0.0, w1, jnp.where(lane == 1.0, w2, 0.0))


def _router(x2d, wr_pad):
    t = x2d.shape[0]
    tr = _tile(t, 512)
    wide = pl.BlockSpec((tr, LANES), lambda i: (i, 0))
    return pl.pallas_call(
        _router_kernel,
        grid=(t // tr,),
        in_specs=[pl.BlockSpec((tr, D_MODEL), lambda i: (i, 0)),
                  pl.BlockSpec((D_MODEL, LANES), lambda i: (0, 0))],
        out_specs=[wide, wide, pl.BlockSpec((SUBLANES, LANES), lambda i: (0, 0))],
        out_shape=[jax.ShapeDtypeStruct((t, LANES), jnp.int32),
                   jax.ShapeDtypeStruct((t, LANES), jnp.float32),
                   jax.ShapeDtypeStruct((SUBLANES, LANES), jnp.float32)],
        scratch_shapes=[pltpu.VMEM((SUBLANES, LANES), jnp.float32)],
        compiler_params=_params("arbitrary"),
        name="router_rank",
    )(x2d, wr_pad)


def _dispatch_kernel(ends_ref, pos_ref, x_ref, xs_ref, zero_ref, sems, zsem, *, tm, td):
    i = pl.program_id(0)
    slot = i & 1

    @pl.when(pl.program_id(0) == 0)
    def _():
        zero_ref[...] = jnp.zeros_like(zero_ref)

        def tile_copy(start):
            return pltpu.make_async_copy(zero_ref, xs_ref.at[pl.ds(pl.multiple_of(start, tm), tm), :], zsem)

        def tail(fn):
            def body(j, carry):
                fn(tile_copy(j * tm))
                return carry
            lax.fori_loop(ends_ref[N_EXPERTS + 1], xs_ref.shape[0] // tm, body, 0)

        for e in range(N_EXPERTS):
            @pl.when(ends_ref[e + 1] > ends_ref[e])
            def _():
                tile_copy(ends_ref[e + 1] - tm).start()
        tail(lambda cp: cp.start())
        for e in range(N_EXPERTS):
            @pl.when(ends_ref[e + 1] > ends_ref[e])
            def _():
                tile_copy(ends_ref[e + 1] - tm).wait()
        tail(lambda cp: cp.wait())

    def start(r, carry):
        for k in range(2):
            p = pos_ref[0, 0, 2 * r + k]
            pltpu.make_async_copy(x_ref.at[pl.ds(i * td + r, 1), :], xs_ref.at[pl.ds(p, 1), :],
                                  sems.at[slot]).start()
        return carry

    def wait_all(s):
        def body(r, carry):
            for k in range(2):
                pltpu.make_async_copy(x_ref.at[pl.ds(0, 1), :], xs_ref.at[pl.ds(0, 1), :], sems.at[s]).wait()
            return carry
        lax.fori_loop(0, td, body, 0, unroll=DMA_UNROLL)

    lax.fori_loop(0, td, start, 0, unroll=DMA_UNROLL)

    @pl.when(i > 0)
    def _():
        wait_all(1 - slot)

    @pl.when(i == pl.num_programs(0) - 1)
    def _():
        wait_all(slot)


def _dispatch(x2d, pos, ends0, n_rows, tm):
    t = x2d.shape[0]
    td = _tile(t, 256)
    grid_spec = pltpu.PrefetchScalarGridSpec(
        num_scalar_prefetch=1,
        grid=(t // td,),
        in_specs=[pl.BlockSpec((1, 1, 2 * td), lambda i, ends: (i, 0, 0), memory_space=pltpu.SMEM),
                  pl.BlockSpec(memory_space=pl.ANY)],
        out_specs=pl.BlockSpec(memory_space=pl.ANY),
        scratch_shapes=[pltpu.VMEM((tm, D_MODEL), jnp.float32),
                        pltpu.SemaphoreType.DMA((2,)), pltpu.SemaphoreType.DMA(())],
    )
    return pl.pallas_call(
        functools.partial(_dispatch_kernel, tm=tm, td=td),
        grid_spec=grid_spec,
        out_shape=jax.ShapeDtypeStruct((n_rows, D_MODEL), jnp.float32),
        compiler_params=_params("arbitrary"),
        name="moe_dispatch",
    )(ends0, pos.reshape(t // td, 1, 2 * td), x2d)


def _moe_ffn_kernel(te_ref, nu_ref, xs_ref, wg_ref, wu_ref, wd_ref, ys_ref):
    i = pl.program_id(0)
    f = pl.program_id(1)

    @pl.when(i < nu_ref[0])
    def _():
        xb = _bf16(xs_ref[...])
        gate = jnp.dot(xb, wg_ref[0], preferred_element_type=jnp.float32)
        up = jnp.dot(xb, wu_ref[0], preferred_element_type=jnp.float32)
        part = jnp.dot(_bf16(gate * _sigmoid(gate) * up), wd_ref[0], preferred_element_type=jnp.float32)

        @pl.when(f == 0)
        def _():
            ys_ref[...] = part

        @pl.when(f > 0)
        def _():
            ys_ref[...] += part

    @pl.when(i >= nu_ref[0])
    def _():
        ys_ref[...] = jnp.zeros_like(ys_ref)


def _moe_ffn(xs, wg, wu, wd, tile_expert, n_used, tm):
    n_rows = xs.shape[0]
    n_tiles = n_rows // tm
    d_ff = wg.shape[2]
    tf = _tile(d_ff, 512)
    nf = d_ff // tf

    def row_idx(i, f, te, nu):
        return (jnp.minimum(i, nu[0] - 1), 0)

    def out_idx(i, f, te, nu):
        return (i, 0)

    def f_idx(i, f, nu):
        return jnp.where(i < nu[0], f, nf - 1)

    def up_idx(i, f, te, nu):
        return (te[jnp.minimum(i, nu[0] - 1)], 0, f_idx(i, f, nu))

    def down_idx(i, f, te, nu):
        return (te[jnp.minimum(i, nu[0] - 1)], f_idx(i, f, nu), 0)

    grid_spec = pltpu.PrefetchScalarGridSpec(
        num_scalar_prefetch=2,
        grid=(n_tiles, nf),
        in_specs=[pl.BlockSpec((tm, D_MODEL), row_idx),
                  pl.BlockSpec((1, D_MODEL, tf), up_idx),
                  pl.BlockSpec((1, D_MODEL, tf), up_idx),
                  pl.BlockSpec((1, tf, D_MODEL), down_idx)],
        out_specs=pl.BlockSpec((tm, D_MODEL), out_idx),
    )
    return pl.pallas_call(
        _moe_ffn_kernel,
        grid_spec=grid_spec,
        out_shape=jax.ShapeDtypeStruct((n_rows, D_MODEL), jnp.float32),
        compiler_params=_params("arbitrary", "arbitrary"),
        name="moe_grouped_ffn",
    )(tile_expert, n_used, xs, wg, wu, wd)


def _combine_kernel(pos_ref, pos_next_ref, gate_ref, x_ref, ys_ref, g_ref, b_ref, o_ref, buf, sems):
    i = pl.program_id(0)
    tc = x_ref.shape[0]
    slot = i & 1

    def issue(p_ref, s):
        def body(r, carry):
            for k in range(2):
                p = p_ref[0, 0, 2 * r + k]
                pltpu.make_async_copy(ys_ref.at[pl.ds(p, 1), :], buf.at[s, k, pl.ds(r, 1), :], sems.at[s]).start()
            return carry
        lax.fori_loop(0, tc, body, 0, unroll=DMA_UNROLL)

    @pl.when(i == 0)
    def _():
        issue(pos_ref, slot)

    @pl.when(i + 1 < pl.num_programs(0))
    def _():
        issue(pos_next_ref, 1 - slot)

    def wait(r, carry):
        for k in range(2):
            pltpu.make_async_copy(ys_ref.at[pl.ds(0, 1), :], buf.at[slot, k, pl.ds(0, 1), :], sems.at[slot]).wait()
        return carry

    lax.fori_loop(0, tc, wait, 0, unroll=DMA_UNROLL)
    gates = gate_ref[...]
    f = gates[:, 0:1] * buf[slot, 0] + gates[:, 1:2] * buf[slot, 1]
    o_ref[...] = _layer_norm(DN_ALPHA * x_ref[...] + f, g_ref[...], b_ref[...])


def _combine(x2d, pos, gates, ys, g, b):
    t = x2d.shape[0]
    tc = _tile(t, 256)
    n = t // tc
    row = pl.BlockSpec((tc, D_MODEL), lambda i: (i, 0))
    vec = pl.BlockSpec((1, D_MODEL), lambda i: (0, 0))
    pos3 = pos.reshape(n, 1, 2 * tc)
    return pl.pallas_call(
        _combine_kernel,
        grid=(n,),
        in_specs=[pl.BlockSpec((1, 1, 2 * tc), lambda i: (i, 0, 0), memory_space=pltpu.SMEM),
                  pl.BlockSpec((1, 1, 2 * tc), lambda i: (jnp.minimum(i + 1, n - 1), 0, 0), memory_space=pltpu.SMEM),
                  pl.BlockSpec((tc, LANES), lambda i: (i, 0)),
                  row,
                  pl.BlockSpec(memory_space=pl.ANY),
                  vec, vec],
        out_specs=row,
        out_shape=jax.ShapeDtypeStruct((t, D_MODEL), jnp.float32),
        scratch_shapes=[pltpu.VMEM((2, 2, tc, D_MODEL), jnp.float32), pltpu.SemaphoreType.DMA((2,))],
        compiler_params=_params("arbitrary"),
        name="moe_combine_ln",
    )(pos3, pos3, gates, x2d, ys, g, b)


def _moe(x2d, wr, wg, wu, wd, g, b):
    t = x2d.shape[0]
    tm = 1024 if t >= 8192 else 128
    wr_pad = jnp.zeros((D_MODEL, LANES), jnp.float32).at[:, :N_EXPERTS].set(wr)
    idx, gates, cnt = _router(x2d, wr_pad)
    counts = cnt[0, :N_EXPERTS].astype(jnp.int32)
    padded = (counts + tm - 1) // tm * tm
    ends = jnp.cumsum(padded)
    starts = ends - padded
    pos = (starts[idx[:, 0:2]] + idx[:, 2:4]).astype(jnp.int32)
    n_tiles = (2 * t) // tm + N_EXPERTS
    tile_start = jnp.arange(n_tiles, dtype=jnp.int32) * tm
    tile_expert = jnp.minimum(jnp.sum(tile_start[:, None] >= ends[None, :], axis=1), N_EXPERTS - 1).astype(jnp.int32)
    n_used = (ends[-1:] // tm).astype(jnp.int32)
    ends0 = jnp.concatenate([jnp.zeros((1,), jnp.int32), ends.astype(jnp.int32), n_used])
    xs = _dispatch(x2d, pos, ends0, n_tiles * tm, tm)
    ys = _moe_ffn(xs, wg, wu, wd, tile_expert, n_used, tm)
    return _combine(x2d, pos, gates, ys, g, b)


def _pool_constants(pool_w):
    n_g = len(POOL_WINDOWS)
    bd = jnp.zeros((MIX_W, MIX_W), jnp.float32)
    for gi in range(n_g):
        sl = slice(gi * POOL_GROUP, (gi + 1) * POOL_GROUP)
        bd = bd.at[sl, sl].set(pool_w[gi])
    win = jnp.repeat(jnp.asarray(POOL_WINDOWS, jnp.float32), POOL_GROUP)[None, :]
    lag = (max(POOL_WINDOWS) - 1 - jnp.arange(max(POOL_WINDOWS), dtype=jnp.float32))[:, None]
    tap = jnp.where(lag < win, 1.0, 0.0).astype(jnp.float32)
    return _bf16(bd), tap, win


def kernel(x, positions, w_in, w_o, pool_w, pool_scale, sconv_w, diff_lambda, diff_subln_g,
           conf_dw, conf_ln_g, conf_ln_b, ln1_g, ln1_b, ln2_g, ln2_b,
           ffn_wg, ffn_wu, ffn_wd, router_w, moe_wg, moe_wu, moe_wd):
    b, s, d = x.shape
    assert d == D_MODEL and s % HALO == 0
    t = b * s
    tabs = _rope_tables(positions)
    x2d = x.reshape(t, d)
    for l in range(DEPTH):
        lam_init = 0.8 - 0.6 * math.exp(-0.3 * l)
        z = _in_proj(x2d, _bf16(w_in[l]))
        poolw_bd, pool_tap, pool_win = _pool_constants(pool_w[l])
        y_pool, y_sconv, y_conf, q_rot, k_rot, vt = _mixers(
            z.reshape(b, s, IN_COLS), tabs, poolw_bd, pool_scale[l][None, :], pool_tap, pool_win,
            sconv_w[l], conf_dw[l], conf_ln_g[l][None, :], conf_ln_b[l][None, :])
        y_attn = _attention(q_rot, k_rot, vt, diff_lambda[l], diff_subln_g[l], lam_init)
        ys = [y.reshape(t, MIX_W) for y in (y_pool, y_sconv, y_attn, y_conf)]
        x2d = _out_proj(ys, x2d, _bf16(w_o[l]), ln1_g[l][None, :], ln1_b[l][None, :])
        j = l // 2
        if l % 2 == 0:
            x2d = _ffn(x2d, _bf16(ffn_wg[j]), _bf16(ffn_wu[j]), _bf16(ffn_wd[j]),
                       ln2_g[l][None, :], ln2_b[l][None, :])
        else:
            x2d = _moe(x2d, router_w[j], _bf16(moe_wg[j]), _bf16(moe_wu[j]), _bf16(moe_wd[j]),
                       ln2_g[l][None, :], ln2_b[l][None, :])
    return x2d.reshape(b, s, d)
```

```python
import functools
import math

import jax
import jax.numpy as jnp
from jax import lax
from jax.experimental import pallas as pl
from jax.experimental.pallas import tpu as pltpu

D_MODEL = 1024
DEPTH = 4
MIX_W = D_MODEL // 4
POOL_WINDOWS = (2, 4, 8, 16)
POOL_GROUP = MIX_W // len(POOL_WINDOWS)
SCONV_K = 3
N_DIFF_HEADS = 4
DIFF_V_DIM = MIX_W // N_DIFF_HEADS
DIFF_QK_DIM = DIFF_V_DIM // 2
ROPE_DIM = DIFF_QK_DIM // 4
ROPE_THETA = 500000.0
CONF_K = 31
N_EXPERTS = 8
DN_ALPHA = (2 * DEPTH) ** 0.25
LN_EPS = 1e-5
IN_COLS = 9 * MIX_W
COL_POOL, COL_SB, COL_SCG, COL_SH, COL_Q, COL_K, COL_V, COL_CA, COL_CG = range(9)

LANES = 128
SUBLANES = 8
VMEM_LIMIT_BYTES = 48 * 1024 * 1024

HALO = 32
CONV_ROWS = 64
ATT_Q_BLOCK = 256
ATT_KV_BLOCK = 512
ATT_PV_K = 256
ATT_HEADS = 2
ATT_CHUNK = 64
DMA_UNROLL = 4
BF16_ROWS = 2 * SUBLANES
V_ROWS = DIFF_V_DIM + BF16_ROWS


def _tile(n, pref):
    if n <= pref:
        return n
    t = pref - pref % SUBLANES
    while t > SUBLANES and n % t:
        t -= SUBLANES
    assert n % t == 0, (n, pref)
    return t


def _params(*sem):
    return pltpu.CompilerParams(dimension_semantics=sem, vmem_limit_bytes=VMEM_LIMIT_BYTES)


def _layer_norm(v, g, b):
    mu = jnp.mean(v, axis=-1, keepdims=True)
    vc = v - mu
    var = jnp.mean(vc * vc, axis=-1, keepdims=True)
    return vc * lax.rsqrt(var + LN_EPS) * g + b


def _sigmoid(v):
    return 1.0 / (1.0 + jnp.exp(-v))


def _bf16(v):
    return v.astype(jnp.bfloat16)


def _in_proj_kernel(x_ref, w_ref, z_ref):
    xb = _bf16(x_ref[...])
    step = 3 * MIX_W
    for c in range(0, IN_COLS, step):
        z_ref[:, c:c + step] = jnp.dot(xb, w_ref[:, c:c + step], preferred_element_type=jnp.float32)


def _in_proj(x2d, w_bf16):
    t = x2d.shape[0]
    tm = _tile(t, 512)
    return pl.pallas_call(
        _in_proj_kernel,
        grid=(t // tm,),
        in_specs=[pl.BlockSpec((tm, D_MODEL), lambda i: (i, 0)),
                  pl.BlockSpec((D_MODEL, IN_COLS), lambda i: (0, 0))],
        out_specs=pl.BlockSpec((tm, IN_COLS), lambda i: (i, 0)),
        out_shape=jax.ShapeDtypeStruct((t, IN_COLS), jnp.float32),
        compiler_params=_params("arbitrary"),
        name="in_proj",
    )(x2d, w_bf16)


def _rope_table_kernel(pos_ref, freq_ref, c_ref, s1_ref, s2_ref):
    ang = pos_ref[0].astype(jnp.float32) * freq_ref[...]
    d = lax.broadcasted_iota(jnp.int32, ang.shape, 1) & (DIFF_QK_DIM - 1)
    cs = jnp.cos(ang)
    sn = jnp.sin(ang)
    half = ROPE_DIM // 2
    c_ref[0] = jnp.where(d < ROPE_DIM, cs, 1.0)
    s1_ref[0] = jnp.where(d < half, -sn, 0.0)
    s2_ref[0] = jnp.where((d >= half) & (d < ROPE_DIM), sn, 0.0)


def _rope_tables(positions):
    b, s = positions.shape
    ts = _tile(s, 512)
    inv = ROPE_THETA ** (-jnp.arange(0, ROPE_DIM, 2, dtype=jnp.float32) / ROPE_DIM)
    d = jnp.arange(LANES) % DIFF_QK_DIM
    freq = jnp.where(d < ROPE_DIM, inv[d % (ROPE_DIM // 2)], 0.0).astype(jnp.float32)[None, :]
    tab = jax.ShapeDtypeStruct((b, s, LANES), jnp.float32)
    spec = pl.BlockSpec((1, ts, LANES), lambda bi, i: (bi, i, 0))
    return pl.pallas_call(
        _rope_table_kernel,
        grid=(b, s // ts),
        in_specs=[pl.BlockSpec((1, ts, 1), lambda bi, i: (bi, i, 0)),
                  pl.BlockSpec((1, LANES), lambda bi, i: (0, 0))],
        out_specs=[spec, spec, spec],
        out_shape=[tab, tab, tab],
        compiler_params=_params("arbitrary", "arbitrary"),
        name="rope_tables",
    )(positions[:, :, None], freq)


def _fill_shifted(buf_ref, residues):
    n = buf_ref.shape[1] - SUBLANES
    for r in residues:
        buf_ref[r, 0:n, :] = buf_ref[0, r:r + n, :]


def _tap_residues(n_taps):
    return sorted({(j - (n_taps - 1)) % SUBLANES for j in range(n_taps)} - {0})


def _causal_taps(buf_ref, w_ref, n_taps, r0):
    acc = None
    for j in range(n_taps):
        start = HALO + r0 - (n_taps - 1) + j
        r = start % SUBLANES
        term = w_ref[j:j + 1, :] * buf_ref[r, start - r:start - r + CONV_ROWS, :]
        acc = term if acc is None else acc + term
    return acc


def _mixer_kernel(p_ref, sb_ref, scg_ref, sh_ref, q_ref, k_ref, v_ref, ca_ref, cg_ref,
                  p_halo, scg_halo, sh_halo, ca_halo, cg_halo,
                  c_ref, s1_ref, s2_ref,
                  poolw_ref, poolscale_ref, pooltap_ref, poolwin_ref,
                  sconvw_ref, confw_ref, confg_ref, confb_ref,
                  ypool_ref, ysconv_ref, yconf_ref, qo_ref, ko_ref, vt_ref,
                  pbuf, gbuf, ubuf, dbuf):
    i = pl.program_id(1)
    ts = p_ref.shape[1]
    keep = jnp.where(i == 0, 0.0, 1.0)

    pbuf[0, 0:HALO, :] = p_halo[0] * keep
    pbuf[0, HALO:, :] = p_ref[0]
    gbuf[0, 0:HALO, :] = scg_halo[0] * sh_halo[0] * keep
    gbuf[0, HALO:, :] = scg_ref[0] * sh_ref[0]
    ubuf[0, 0:HALO, :] = ca_halo[0] * _sigmoid(cg_halo[0]) * keep
    ubuf[0, HALO:, :] = ca_ref[0] * _sigmoid(cg_ref[0])
    n_pool_taps = max(POOL_WINDOWS)
    _fill_shifted(pbuf, _tap_residues(n_pool_taps))
    _fill_shifted(gbuf, _tap_residues(SCONV_K))
    _fill_shifted(ubuf, _tap_residues(CONF_K))

    for r0 in range(0, ts, CONV_ROWS):
        rows = slice(r0, r0 + CONV_ROWS)
        wsum = _causal_taps(pbuf, pooltap_ref, n_pool_taps, r0)
        tpos = (i * ts + r0 + 1 + lax.broadcasted_iota(jnp.int32, (CONV_ROWS, MIX_W), 0)).astype(jnp.float32)
        cnt = jnp.minimum(tpos, poolwin_ref[...])
        dbuf[rows, :] = wsum / cnt - pbuf[0, HALO + r0:HALO + r0 + CONV_ROWS, :]
        sc = _causal_taps(gbuf, sconvw_ref, SCONV_K, r0)
        ysconv_ref[0, rows, :] = _bf16(sb_ref[0, rows, :] * sc)
        u = _causal_taps(ubuf, confw_ref, CONF_K, r0)
        u = _layer_norm(u, confg_ref[...], confb_ref[...])
        yconf_ref[0, rows, :] = _bf16(u * _sigmoid(u))

    y = jnp.dot(_bf16(dbuf[...]), poolw_ref[...], preferred_element_type=jnp.float32)
    ypool_ref[0] = _bf16(y * poolscale_ref[...])

    c, s1, s2 = c_ref[0], s1_ref[0], s2_ref[0]
    half = ROPE_DIM // 2
    for src, dst, scale in ((q_ref, qo_ref, DIFF_QK_DIM ** -0.5 * math.log2(math.e)), (k_ref, ko_ref, None)):
        for lo in range(0, MIX_W, LANES):
            xh = src[0, :, lo:lo + LANES]
            r = xh * c + pltpu.roll(xh, LANES - half, 1) * s1 + pltpu.roll(xh, half, 1) * s2
            if scale is not None:
                r = r * scale
            dst[0, :, lo:lo + LANES] = _bf16(r)
    vt = v_ref[0].T
    ones_row = jnp.where(lax.broadcasted_iota(jnp.int32, (BF16_ROWS, ts), 0) == 0, 1.0, 0.0)
    for h in range(N_DIFF_HEADS):
        vt_ref[0, 0, h * V_ROWS:h * V_ROWS + DIFF_V_DIM, :] = _bf16(vt[h * DIFF_V_DIM:(h + 1) * DIFF_V_DIM, :])
        vt_ref[0, 0, h * V_ROWS + DIFF_V_DIM:(h + 1) * V_ROWS, :] = _bf16(ones_row)


def _mixers(z3, tabs, poolw_bd, pool_scale, pool_tap, pool_win, sconv_w, conf_dw, conf_g, conf_b):
    b, s, _ = z3.shape
    ts = ATT_KV_BLOCK if s % ATT_KV_BLOCK == 0 else s
    nblk = s // ts
    per_halo = ts // HALO

    def col(cidx):
        return pl.BlockSpec((1, ts, MIX_W), lambda bi, i: (bi, i, cidx))

    def halo(cidx):
        return pl.BlockSpec((1, HALO, MIX_W), lambda bi, i: (bi, jnp.maximum(i * per_halo - 1, 0), cidx))

    def const(shape):
        return pl.BlockSpec(shape, lambda bi, i: (0,) * len(shape))

    tab_spec = pl.BlockSpec((1, ts, LANES), lambda bi, i: (bi, i, 0))
    y_spec = pl.BlockSpec((1, ts, MIX_W), lambda bi, i: (bi, i, 0))
    y_shape = jax.ShapeDtypeStruct((b, s, MIX_W), jnp.bfloat16)
    return pl.pallas_call(
        _mixer_kernel,
        grid=(b, nblk),
        in_specs=[col(c) for c in range(9)]
        + [halo(c) for c in (COL_POOL, COL_SCG, COL_SH, COL_CA, COL_CG)]
        + [tab_spec, tab_spec, tab_spec]
        + [const((MIX_W, MIX_W)), const((1, MIX_W)), const((max(POOL_WINDOWS), MIX_W)), const((1, MIX_W)),
           const((SCONV_K, MIX_W)), const((CONF_K, MIX_W)), const((1, MIX_W)), const((1, MIX_W))],
        out_specs=[y_spec, y_spec, y_spec, y_spec, y_spec,
                   pl.BlockSpec((1, 1, N_DIFF_HEADS * V_ROWS, ts), lambda bi, i: (bi, i, 0, 0))],
        out_shape=[y_shape, y_shape, y_shape, y_shape, y_shape,
                   jax.ShapeDtypeStruct((b, nblk, N_DIFF_HEADS * V_ROWS, ts), jnp.bfloat16)],
        scratch_shapes=[pltpu.VMEM((SUBLANES, HALO + ts, MIX_W), jnp.float32)] * 3
        + [pltpu.VMEM((ts, MIX_W), jnp.float32)],
        compiler_params=_params("arbitrary", "arbitrary"),
        name="mixers",
    )(*([z3] * 14), *tabs, poolw_bd, pool_scale, pool_tap, pool_win, sconv_w, conf_dw, conf_g, conf_b)


def _attn_kernel(lam_ref, g_ref, q_ref, k_ref, vt_ref, o_ref, qz_sc, sa_sc, sb_sc, ca_sc, cb_sc,
                 m_sc, acc_sc, *, out_scale, lam_init):
    qi = pl.program_id(2)
    tq = q_ref.shape[1]
    q = q_ref[0]
    n_hm = q.shape[1] // DIFF_QK_DIM
    lane = lax.broadcasted_iota(jnp.int32, q.shape, 1)
    for hm in range(n_hm):
        lo = hm * DIFF_QK_DIM
        qz_sc[hm] = jnp.where((lane >= lo) & (lane < lo + DIFF_QK_DIM), q, jnp.zeros_like(q))

    m_sc[...] = jnp.full(m_sc.shape, -jnp.inf, jnp.float32)
    acc_sc[...] = jnp.zeros(acc_sc.shape, jnp.float32)

    tk = vt_ref.shape[3]
    per_kv = tk // tq
    diag = qi // per_kv
    q_off = (qi - diag * per_kv) * tq

    def produce(kb, s_buf, c_buf, diagonal):
        kt = k_ref[0, pl.ds(pl.multiple_of(kb * tk, tk), tk), :]
        for hm in range(n_hm):
            s = lax.dot_general(kt, qz_sc[hm], (((1,), (1,)), ((), ())),
                                preferred_element_type=jnp.float32)
            if diagonal:
                kr = lax.broadcasted_iota(jnp.int32, s.shape, 0)
                qc = lax.broadcasted_iota(jnp.int32, s.shape, 1)
                s = jnp.where(kr <= qc + q_off, s, -jnp.inf)
            s_buf[hm] = s
            c_buf[hm] = jnp.max(s, axis=0, keepdims=True)

    def consume(kb, s_buf, c_buf):
        vt = vt_ref[0, kb]
        for hm in range(n_hm):
            m_old = m_sc[hm]
            m_new = jnp.maximum(m_old, c_buf[hm])
            a = jnp.exp2(m_old - m_new)
            h = hm // 2
            pv = None
            for k0 in range(0, tk, ATT_PV_K):
                p = jnp.concatenate([_bf16(jnp.exp2(s_buf[hm, c:c + ATT_CHUNK, :] - m_new))
                                     for c in range(k0, k0 + ATT_PV_K, ATT_CHUNK)], axis=0)
                part = jnp.dot(vt[h * V_ROWS:(h + 1) * V_ROWS, k0:k0 + ATT_PV_K], p,
                               preferred_element_type=jnp.float32)
                pv = part if pv is None else pv + part
            acc_sc[hm] = acc_sc[hm] * a + pv
            m_sc[hm] = m_new

    produce(diag, sa_sc, ca_sc, True)
    pairs = lax.shift_right_logical(diag, 1)

    def body(t, carry):
        produce(2 * t, sb_sc, cb_sc, False)
        consume(jnp.where(t == 0, diag, 2 * t - 1), sa_sc, ca_sc)
        produce(2 * t + 1, sa_sc, ca_sc, False)
        consume(2 * t, sb_sc, cb_sc)
        return carry

    lax.fori_loop(0, pairs, body, 0)
    odd = (diag & 1) == 1

    @pl.when(odd)
    def _():
        produce(diag - 1, sb_sc, cb_sc, False)

    consume(jnp.where(pairs == 0, diag, 2 * pairs - 1), sa_sc, ca_sc)

    @pl.when(odd)
    def _():
        consume(diag - 1, sb_sc, cb_sc)

    lp = lam_ref[...]
    lam = (jnp.exp(jnp.sum(lp[0:1] * lp[1:2], axis=1, keepdims=True))
           - jnp.exp(jnp.sum(lp[2:3] * lp[3:4], axis=1, keepdims=True)) + lam_init)
    outs = []
    def normalised(hm):
        return acc_sc[hm, 0:DIFF_V_DIM, :] / acc_sc[hm, DIFF_V_DIM:DIFF_V_DIM + 1, :]

    for h in range(n_hm // 2):
        o = normalised(2 * h) - lam * normalised(2 * h + 1)
        ms = jnp.mean(o * o, axis=0, keepdims=True)
        outs.append(o * lax.rsqrt(ms + LN_EPS) * g_ref[...] * out_scale)
    o_ref[0] = _bf16(jnp.concatenate(outs, axis=0).T)


def _attention(q_rot, k_rot, vt, lam_params, subln_g, lam_init):
    b, s, _ = q_rot.shape
    nkv, tk = vt.shape[1], vt.shape[3]
    tq = min(ATT_Q_BLOCK, tk)
    assert tk % tq == 0 and tk % ATT_PV_K == 0
    width = ATT_HEADS * DIFF_V_DIM
    n_hm = 2 * ATT_HEADS
    kern = functools.partial(_attn_kernel, out_scale=1.0 - lam_init, lam_init=lam_init)
    return pl.pallas_call(
        kern,
        grid=(b, MIX_W // width, s // tq),
        in_specs=[pl.BlockSpec((4, DIFF_QK_DIM), lambda bi, hp, qi: (0, 0)),
                  pl.BlockSpec((DIFF_V_DIM, 1), lambda bi, hp, qi: (0, 0)),
                  pl.BlockSpec((1, tq, width), lambda bi, hp, qi: (bi, qi, hp)),
                  pl.BlockSpec((1, s, width), lambda bi, hp, qi: (bi, 0, hp)),
                  pl.BlockSpec((1, nkv, ATT_HEADS * V_ROWS, tk), lambda bi, hp, qi: (bi, 0, hp, 0))],
        out_specs=pl.BlockSpec((1, tq, width), lambda bi, hp, qi: (bi, qi, hp)),
        out_shape=jax.ShapeDtypeStruct((b, s, MIX_W), jnp.bfloat16),
        scratch_shapes=[pltpu.VMEM((n_hm, tq, width), jnp.bfloat16),
                        pltpu.VMEM((n_hm, tk, tq), jnp.float32), pltpu.VMEM((n_hm, tk, tq), jnp.float32),
                        pltpu.VMEM((n_hm, 1, tq), jnp.float32), pltpu.VMEM((n_hm, 1, tq), jnp.float32),
                        pltpu.VMEM((n_hm, 1, tq), jnp.float32),
                        pltpu.VMEM((n_hm, V_ROWS, tq), jnp.float32)],
        compiler_params=_params("arbitrary", "arbitrary", "arbitrary"),
        name="diff_attention",
    )(lam_params, subln_g[:, None], q_rot, k_rot, vt)


def _out_proj_kernel(yp_ref, ys_ref, ya_ref, yc_ref, x_ref, w_ref, g_ref, b_ref, o_ref):
    acc = None
    for gi, y_ref in enumerate((yp_ref, ys_ref, ya_ref, yc_ref)):
        part = jnp.dot(y_ref[...], w_ref[gi * MIX_W:(gi + 1) * MIX_W, :], preferred_element_type=jnp.float32)
        acc = part if acc is None else acc + part
    o_ref[...] = _layer_norm(DN_ALPHA * x_ref[...] + acc, g_ref[...], b_ref[...])


def _out_proj(ys, x2d, w_bf16, g, b):
    t = x2d.shape[0]
    tm = _tile(t, 512)
    y_spec = pl.BlockSpec((tm, MIX_W), lambda i: (i, 0))
    row = pl.BlockSpec((tm, D_MODEL), lambda i: (i, 0))
    vec = pl.BlockSpec((1, D_MODEL), lambda i: (0, 0))
    return pl.pallas_call(
        _out_proj_kernel,
        grid=(t // tm,),
        in_specs=[y_spec] * 4 + [row, pl.BlockSpec((D_MODEL, D_MODEL), lambda i: (0, 0)), vec, vec],
        out_specs=row,
        out_shape=jax.ShapeDtypeStruct((t, D_MODEL), jnp.float32),
        compiler_params=_params("arbitrary"),
        name="out_proj_ln",
    )(*ys, x2d, w_bf16, g, b)


def _ffn_kernel(x_ref, wg_ref, wu_ref, wd_ref, g_ref, b_ref, o_ref):
    xb = _bf16(x_ref[...])
    gate = jnp.dot(xb, wg_ref[...], preferred_element_type=jnp.float32)
    up = jnp.dot(xb, wu_ref[...], preferred_element_type=jnp.float32)
    f = jnp.dot(_bf16(gate * _sigmoid(gate) * up), wd_ref[...], preferred_element_type=jnp.float32)
    o_ref[...] = _layer_norm(DN_ALPHA * x_ref[...] + f, g_ref[...], b_ref[...])


def _ffn(x2d, wg, wu, wd, g, b):
    t = x2d.shape[0]
    d_ff = wg.shape[1]
    tm = _tile(t, 512)
    row = pl.BlockSpec((tm, D_MODEL), lambda i: (i, 0))
    vec = pl.BlockSpec((1, D_MODEL), lambda i: (0, 0))
    once = pl.Buffered(1)
    return pl.pallas_call(
        _ffn_kernel,
        grid=(t // tm,),
        in_specs=[row,
                  pl.BlockSpec((D_MODEL, d_ff), lambda i: (0, 0), pipeline_mode=once),
                  pl.BlockSpec((D_MODEL, d_ff), lambda i: (0, 0), pipeline_mode=once),
                  pl.BlockSpec((d_ff, D_MODEL), lambda i: (0, 0), pipeline_mode=once),
                  vec, vec],
        out_specs=row,
        out_shape=jax.ShapeDtypeStruct((t, D_MODEL), jnp.float32),
        compiler_params=_params("arbitrary"),
        name="ffn_ln",
    )(x2d, wg, wu, wd, g, b)


def _router_kernel(x_ref, wr_ref, idx_ref, gate_ref, cnt_ref, carry_ref):
    @pl.when(pl.program_id(0) == 0)
    def _():
        carry_ref[...] = jnp.zeros_like(carry_ref)

    tr = x_ref.shape[0]
    logits = jnp.dot(x_ref[...], wr_ref[...], preferred_element_type=jnp.float32,
                     precision=lax.Precision.HIGHEST)
    lane = lax.broadcasted_iota(jnp.int32, logits.shape, 1).astype(jnp.float32)
    lg = jnp.where(lane < N_EXPERTS, logits, -jnp.inf)
    m1 = jnp.max(lg, axis=1, keepdims=True)
    i1 = jnp.min(jnp.where(lg == m1, lane, float(LANES)), axis=1, keepdims=True)
    lg2 = jnp.where(lane == i1, -jnp.inf, lg)
    m2 = jnp.max(lg2, axis=1, keepdims=True)
    i2 = jnp.min(jnp.where(lg2 == m2, lane, float(LANES)), axis=1, keepdims=True)
    e2 = jnp.exp(m2 - m1)
    den = 1.0 + e2
    w1 = 1.0 / den
    w2 = e2 / den

    sel1 = lane == i1
    sel2 = lane == i2
    onehot = jnp.where(sel1 | sel2, 1.0, 0.0)
    rr = lax.broadcasted_iota(jnp.int32, (tr, tr), 0)
    cc = lax.broadcasted_iota(jnp.int32, (tr, tr), 1)
    lower = _bf16(jnp.where(cc < rr, 1.0, 0.0))
    carry = carry_ref[0:1, :]
    before = jnp.dot(lower, _bf16(onehot), preferred_element_type=jnp.float32) + carry
    r1 = jnp.sum(jnp.where(sel1, before, 0.0), axis=1, keepdims=True)
    r2 = jnp.sum(jnp.where(sel2, before, 0.0), axis=1, keepdims=True)
    carry_ref[...] = jnp.broadcast_to(carry + jnp.sum(onehot, axis=0, keepdims=True), carry_ref.shape)
    cnt_ref[...] = carry_ref[...]
    info = jnp.where(lane == 0.0, i1, jnp.where(lane == 1.0, i2,
                     jnp.where(lane == 2.0, r1, jnp.where(lane == 3.0, r2, 0.0))))
    idx_ref[...] = info.astype(jnp.int32)
    gate_ref[...] = jnp.where(lane == 0.0, w1, jnp.where(lane == 1.0, w2, 0.0))


def _router(x2d, wr_pad):
    t = x2d.shape[0]
    tr = _tile(t, 512)
    wide = pl.BlockSpec((tr, LANES), lambda i: (i, 0))
    return pl.pallas_call(
        _router_kernel,
        grid=(t // tr,),
        in_specs=[pl.BlockSpec((tr, D_MODEL), lambda i: (i, 0)),
                  pl.BlockSpec((D_MODEL, LANES), lambda i: (0, 0))],
        out_specs=[wide, wide, pl.BlockSpec((SUBLANES, LANES), lambda i: (0, 0))],
        out_shape=[jax.ShapeDtypeStruct((t, LANES), jnp.int32),
                   jax.ShapeDtypeStruct((t, LANES), jnp.float32),
                   jax.ShapeDtypeStruct((SUBLANES, LANES), jnp.float32)],
        scratch_shapes=[pltpu.VMEM((SUBLANES, LANES), jnp.float32)],
        compiler_params=_params("arbitrary"),
        name="router_rank",
    )(x2d, wr_pad)


def _dispatch_kernel(ends_ref, pos_ref, x_ref, xs_ref, zero_ref, sem, zsem, *, tm):
    td = x_ref.shape[0]

    @pl.when(pl.program_id(0) == 0)
    def _():
        zero_ref[...] = jnp.zeros_like(zero_ref)

        def tile_copy(start):
            return pltpu.make_async_copy(zero_ref, xs_ref.at[pl.ds(pl.multiple_of(start, tm), tm), :], zsem)

        def tail(fn):
            def body(j, carry):
                fn(tile_copy(j * tm))
                return carry
            lax.fori_loop(ends_ref[N_EXPERTS + 1], xs_ref.shape[0] // tm, body, 0)

        for e in range(N_EXPERTS):
            @pl.when(ends_ref[e + 1] > ends_ref[e])
            def _():
                tile_copy(ends_ref[e + 1] - tm).start()
        tail(lambda cp: cp.start())
        for e in range(N_EXPERTS):
            @pl.when(ends_ref[e + 1] > ends_ref[e])
            def _():
                tile_copy(ends_ref[e + 1] - tm).wait()
        tail(lambda cp: cp.wait())

    def row_copy(r, k):
        p = pos_ref[0, 0, 2 * r + k]
        return pltpu.make_async_copy(x_ref.at[pl.ds(r, 1), :], xs_ref.at[pl.ds(p, 1), :], sem)

    def start(r, carry):
        row_copy(r, 0).start()
        row_copy(r, 1).start()
        return carry

    def wait(r, carry):
        row_copy(r, 0).wait()
        row_copy(r, 1).wait()
        return carry

    lax.fori_loop(0, td, start, 0)
    lax.fori_loop(0, td, wait, 0)


def _dispatch(x2d, pos, ends0, n_rows, tm):
    t = x2d.shape[0]
    td = _tile(t, 256)
    grid_spec = pltpu.PrefetchScalarGridSpec(
        num_scalar_prefetch=1,
        grid=(t // td,),
        in_specs=[pl.BlockSpec((1, 1, 2 * td), lambda i, ends: (i, 0, 0), memory_space=pltpu.SMEM),
                  pl.BlockSpec((td, D_MODEL), lambda i, ends: (i, 0))],
        out_specs=pl.BlockSpec(memory_space=pl.ANY),
        scratch_shapes=[pltpu.VMEM((tm, D_MODEL), jnp.float32),
                        pltpu.SemaphoreType.DMA(()), pltpu.SemaphoreType.DMA(())],
    )
    return pl.pallas_call(
        functools.partial(_dispatch_kernel, tm=tm),
        grid_spec=grid_spec,
        out_shape=jax.ShapeDtypeStruct((n_rows, D_MODEL), jnp.float32),
        compiler_params=_params("arbitrary"),
        name="moe_dispatch",
    )(ends0, pos.reshape(t // td, 1, 2 * td), x2d)


def _moe_ffn_kernel(te_ref, nu_ref, xs_ref, wg_ref, wu_ref, wd_ref, ys_ref):
    i = pl.program_id(0)
    f = pl.program_id(1)

    @pl.when(i < nu_ref[0])
    def _():
        xb = _bf16(xs_ref[...])
        gate = jnp.dot(xb, wg_ref[0], preferred_element_type=jnp.float32)
        up = jnp.dot(xb, wu_ref[0], preferred_element_type=jnp.float32)
        part = jnp.dot(_bf16(gate * _sigmoid(gate) * up), wd_ref[0], preferred_element_type=jnp.float32)

        @pl.when(f == 0)
        def _():
            ys_ref[...] = part

        @pl.when(f > 0)
        def _():
            ys_ref[...] += part

    @pl.when(i >= nu_ref[0])
    def _():
        ys_ref[...] = jnp.zeros_like(ys_ref)


def _moe_ffn(xs, wg, wu, wd, tile_expert, n_used, tm):
    n_rows = xs.shape[0]
    n_tiles = n_rows // tm
    d_ff = wg.shape[2]
    tf = _tile(d_ff, 1792)
    nf = d_ff // tf

    def row_idx(i, f, te, nu):
        return (jnp.minimum(i, nu[0] - 1), 0)

    def out_idx(i, f, te, nu):
        return (i, 0)

    def f_idx(i, f, nu):
        return jnp.where(i < nu[0], f, nf - 1)

    def up_idx(i, f, te, nu):
        return (te[jnp.minimum(i, nu[0] - 1)], 0, f_idx(i, f, nu))

    def down_idx(i, f, te, nu):
        return (te[jnp.minimum(i, nu[0] - 1)], f_idx(i, f, nu), 0)

    grid_spec = pltpu.PrefetchScalarGridSpec(
        num_scalar_prefetch=2,
        grid=(n_tiles, nf),
        in_specs=[pl.BlockSpec((tm, D_MODEL), row_idx),
                  pl.BlockSpec((1, D_MODEL, tf), up_idx),
                  pl.BlockSpec((1, D_MODEL, tf), up_idx),
                  pl.BlockSpec((1, tf, D_MODEL), down_idx)],
        out_specs=pl.BlockSpec((tm, D_MODEL), out_idx),
    )
    return pl.pallas_call(
        _moe_ffn_kernel,
        grid_spec=grid_spec,
        out_shape=jax.ShapeDtypeStruct((n_rows, D_MODEL), jnp.float32),
        compiler_params=_params("arbitrary", "arbitrary"),
        name="moe_grouped_ffn",
    )(tile_expert, n_used, xs, wg, wu, wd)


def _combine_kernel(pos_ref, pos_next_ref, gate_ref, x_ref, ys_ref, g_ref, b_ref, o_ref, buf, sems):
    i = pl.program_id(0)
    tc = x_ref.shape[0]
    slot = i & 1

    def issue(p_ref, s):
        def body(r, carry):
            for k in range(2):
                p = p_ref[0, 0, 2 * r + k]
                pltpu.make_async_copy(ys_ref.at[pl.ds(p, 1), :], buf.at[s, k, pl.ds(r, 1), :], sems.at[s]).start()
            return carry
        lax.fori_loop(0, tc, body, 0, unroll=DMA_UNROLL)

    @pl.when(i == 0)
    def _():
        issue(pos_ref, slot)

    @pl.when(i + 1 < pl.num_programs(0))
    def _():
        issue(pos_next_ref, 1 - slot)

    def wait(r, carry):
        for k in range(2):
            pltpu.make_async_copy(ys_ref.at[pl.ds(0, 1), :], buf.at[slot, k, pl.ds(0, 1), :], sems.at[slot]).wait()
        return carry

    lax.fori_loop(0, tc, wait, 0, unroll=DMA_UNROLL)
    gates = gate_ref[...]
    f = gates[:, 0:1] * buf[slot, 0] + gates[:, 1:2] * buf[slot, 1]
    o_ref[...] = _layer_norm(DN_ALPHA * x_ref[...] + f, g_ref[...], b_ref[...])


def _combine(x2d, pos, gates, ys, g, b):
    t = x2d.shape[0]
    tc = _tile(t, 256)
    n = t // tc
    row = pl.BlockSpec((tc, D_MODEL), lambda i: (i, 0))
    vec = pl.BlockSpec((1, D_MODEL), lambda i: (0, 0))
    pos3 = pos.reshape(n, 1, 2 * tc)
    return pl.pallas_call(
        _combine_kernel,
        grid=(n,),
        in_specs=[pl.BlockSpec((1, 1, 2 * tc), lambda i: (i, 0, 0), memory_space=pltpu.SMEM),
                  pl.BlockSpec((1, 1, 2 * tc), lambda i: (jnp.minimum(i + 1, n - 1), 0, 0), memory_space=pltpu.SMEM),
                  pl.BlockSpec((tc, LANES), lambda i: (i, 0)),
                  row,
                  pl.BlockSpec(memory_space=pl.ANY),
                  vec, vec],
        out_specs=row,
        out_shape=jax.ShapeDtypeStruct((t, D_MODEL), jnp.float32),
        scratch_shapes=[pltpu.VMEM((2, 2, tc, D_MODEL), jnp.float32), pltpu.SemaphoreType.DMA((2,))],
        compiler_params=_params("arbitrary"),
        name="moe_combine_ln",
    )(pos3, pos3, gates, x2d, ys, g, b)


def _moe(x2d, wr, wg, wu, wd, g, b):
    t = x2d.shape[0]
    tm = 512 if t >= 8192 else 128
    wr_pad = jnp.zeros((D_MODEL, LANES), jnp.float32).at[:, :N_EXPERTS].set(wr)
    idx, gates, cnt = _router(x2d, wr_pad)
    counts = cnt[0, :N_EXPERTS].astype(jnp.int32)
    padded = (counts + tm - 1) // tm * tm
    ends = jnp.cumsum(padded)
    starts = ends - padded
    pos = (starts[idx[:, 0:2]] + idx[:, 2:4]).astype(jnp.int32)
    n_tiles = (2 * t) // tm + N_EXPERTS
    tile_start = jnp.arange(n_tiles, dtype=jnp.int32) * tm
    tile_expert = jnp.minimum(jnp.sum(tile_start[:, None] >= ends[None, :], axis=1), N_EXPERTS - 1).astype(jnp.int32)
    n_used = (ends[-1:] // tm).astype(jnp.int32)
    ends0 = jnp.concatenate([jnp.zeros((1,), jnp.int32), ends.astype(jnp.int32), n_used])
    xs = _dispatch(x2d, pos, ends0, n_tiles * tm, tm)
    ys = _moe_ffn(xs, wg, wu, wd, tile_expert, n_used, tm)
    return _combine(x2d, pos, gates, ys, g, b)


def _pool_constants(pool_w):
    n_g = len(POOL_WINDOWS)
    bd = jnp.zeros((MIX_W, MIX_W), jnp.float32)
    for gi in range(n_g):
        sl = slice(gi * POOL_GROUP, (gi + 1) * POOL_GROUP)
        bd = bd.at[sl, sl].set(pool_w[gi])
    win = jnp.repeat(jnp.asarray(POOL_WINDOWS, jnp.float32), POOL_GROUP)[None, :]
    lag = (max(POOL_WINDOWS) - 1 - jnp.arange(max(POOL_WINDOWS), dtype=jnp.float32))[:, None]
    tap = jnp.where(lag < win, 1.0, 0.0).astype(jnp.float32)
    return _bf16(bd), tap, win


def kernel(x, positions, w_in, w_o, pool_w, pool_scale, sconv_w, diff_lambda, diff_subln_g,
           conf_dw, conf_ln_g, conf_ln_b, ln1_g, ln1_b, ln2_g, ln2_b,
           ffn_wg, ffn_wu, ffn_wd, router_w, moe_wg, moe_wu, moe_wd):
    b, s, d = x.shape
    assert d == D_MODEL and s % HALO == 0
    t = b * s
    tabs = _rope_tables(positions)
    x2d = x.reshape(t, d)
    for l in range(DEPTH):
        lam_init = 0.8 - 0.6 * math.exp(-0.3 * l)
        z = _in_proj(x2d, _bf16(w_in[l]))
        poolw_bd, pool_tap, pool_win = _pool_constants(pool_w[l])
        y_pool, y_sconv, y_conf, q_rot, k_rot, vt = _mixers(
            z.reshape(b, s, IN_COLS), tabs, poolw_bd, pool_scale[l][None, :], pool_tap, pool_win,
            sconv_w[l], conf_dw[l], conf_ln_g[l][None, :], conf_ln_b[l][None, :])
        y_attn = _attention(q_rot, k_rot, vt, diff_lambda[l], diff_subln_g[l], lam_init)
        ys = [y.reshape(t, MIX_W) for y in (y_pool, y_sconv, y_attn, y_conf)]
        x2d = _out_proj(ys, x2d, _bf16(w_o[l]), ln1_g[l][None, :], ln1_b[l][None, :])
        j = l // 2
        if l % 2 == 0:
            x2d = _ffn(x2d, _bf16(ffn_wg[j]), _bf16(ffn_wu[j]), _bf16(ffn_wd[j]),
                       ln2_g[l][None, :], ln2_b[l][None, :])
        else:
            x2d = _moe(x2d, router_w[j], _bf16(moe_wg[j]), _bf16(moe_wu[j]), _bf16(moe_wd[j]),
                       ln2_g[l][None, :], ln2_b[l][None, :])
    return x2d.reshape(b, s, d)
```
